```python
import math
import jax, jax.numpy as jnp
from jax import lax
import numpy as np

D_MODEL = 1024
BATCH = 4
SEQ = 4096
DEPTH = 1

MEM_LEN = 256
CHUNK = 128
G_HEADS = 4
G_WIDTH = D_MODEL // 4
G_DIM = G_WIDTH // G_HEADS
DA_HEADS = 4
DA_WIDTH = D_MODEL // 2
DA_V_DIM = DA_WIDTH // DA_HEADS
DA_HEAD_DIM = DA_V_DIM // 2
DA_QK = DA_HEADS * 2 * DA_HEAD_DIM
M_HEADS = 4
M_WIDTH = D_MODEL // 4
M_HEAD_DIM = M_WIDTH // M_HEADS
MIX_WIDTH = G_WIDTH + DA_WIDTH + M_WIDTH
IN_WIDTH = 2 * G_WIDTH + 2 * DA_QK + DA_WIDTH + M_WIDTH
SPLITS = (G_WIDTH, 2 * G_WIDTH, 2 * G_WIDTH + DA_QK, 2 * G_WIDTH + 2 * DA_QK,
          2 * G_WIDTH + 2 * DA_QK + DA_WIDTH)
D_FF = 2816
CONV_W = 3
Q_BLOCK = 128
LN_EPS = 1e-5
ALPHA = (2.0 * DEPTH) ** 0.25
BETA = (8.0 * DEPTH) ** -0.25

kernel_name = "hymba_style_gmlp_diffattn_mem_deepnorm_encoder"


def layer_norm(x, g, b):
    xf = x.astype(jnp.float32)
    mu = jnp.mean(xf, axis=-1, keepdims=True)
    var = jnp.mean(jnp.square(xf - mu), axis=-1, keepdims=True)
    return ((xf - mu) * lax.rsqrt(var + LN_EPS)).astype(x.dtype) * g + b


def rms_norm(x, g):
    xf = x.astype(jnp.float32)
    ms = jnp.mean(jnp.square(xf), axis=-1, keepdims=True)
    return (xf * lax.rsqrt(ms + LN_EPS)).astype(x.dtype) * g


def alibi_slopes(n):
    return np.array([2.0 ** (-8.0 * (i + 1) / n) for i in range(n)], dtype=np.float32)


def gmlp_group(u, v, ws, bs, ln_g, ln_b):
    B, T, _ = u.shape
    u = jax.nn.gelu(u, approximate=False)
    v = layer_norm(jax.nn.gelu(v, approximate=False), ln_g, ln_b)
    vc = v.reshape(B, T // CHUNK, CHUNK, G_HEADS, G_DIM)
    s = jnp.einsum('gts,bcsgd->bctgd', ws, vc) + bs.T[None, None, :, :, None]
    return u * s.reshape(B, T, G_WIDTH)


def diff_attention(q, k, v, lam, slopes):
    B, T, H, _, dk = q.shape
    dv = v.shape[-1]
    nb = T // Q_BLOCK
    qb = (q * (dk ** -0.5)).reshape(B, nb, Q_BLOCK, H, 2, dk).transpose(1, 0, 2, 3, 4, 5)
    kpos = jnp.arange(T, dtype=jnp.float32)

    def block(args):
        qi, i = args
        s = jnp.einsum('bqhmd,bkhmd->bhmqk', qi, k).astype(jnp.float32)
        qpos = (i * Q_BLOCK + jnp.arange(Q_BLOCK)).astype(jnp.float32)
        dist = jnp.abs(qpos[:, None] - kpos[None, :])
        s = s - slopes[None, :, None, None, None] * dist
        p = jax.nn.softmax(s, axis=-1)
        a = p[:, :, 0] - lam * p[:, :, 1]
        return jnp.einsum('bhqk,bkhd->bqhd', a.astype(v.dtype), v)

    out = lax.map(block, (qb, jnp.arange(nb)))
    return out.transpose(1, 0, 2, 3, 4).reshape(B, T, H, dv)


def memory_attention(qm, kv):
    B, M, _ = kv.shape
    km, vm = jnp.split(kv, 2, axis=-1)
    km = km.reshape(B, M, M_HEADS, M_HEAD_DIM)
    vm = vm.reshape(B, M, M_HEADS, M_HEAD_DIM)
    s = jnp.einsum('bthd,bmhd->bhtm', qm * (M_HEAD_DIM ** -0.5), km).astype(jnp.float32)
    p = jax.nn.softmax(s, axis=-1)
    o = jnp.einsum('bhtm,bmhd->bthd', p.astype(vm.dtype), vm)
    return o.reshape(qm.shape[0], qm.shape[1], M_WIDTH)


def conv_ffn(h, w_up, conv_w, conv_b, w_down):
    a = h @ w_up
    c = a.shape[-1]
    a = lax.conv_general_dilated(
        a, conv_w[:, None, :].astype(a.dtype), window_strides=(1,),
        padding=((CONV_W // 2, CONV_W // 2),),
        dimension_numbers=('NWC', 'WIO', 'NWC'), feature_group_count=c) + conv_b
    gate, val = jnp.split(a, 2, axis=-1)
    return (jax.nn.gelu(gate, approximate=False) * val) @ w_down


def setup_inputs(seed: int = 0) -> dict:
    key = jax.random.key(seed)
    ks = jax.random.split(key, 32)
    f32 = jnp.float32
    L, D = DEPTH, D_MODEL
    nrm = lambda k, shape, s: jax.random.normal(k, shape, f32) * s
    return {
        "x": nrm(ks[0], (BATCH, SEQ, D), 1.0),
        "mem": nrm(ks[1], (BATCH, MEM_LEN, D), 1.0),
        "ln_emb_g": 1.0 + nrm(ks[2], (D,), 0.01),
        "ln_emb_b": nrm(ks[3], (D,), 0.01),
        "w_in": nrm(ks[4], (L, D, IN_WIDTH), D ** -0.5),
        "gmlp_ln_g": 1.0 + nrm(ks[5], (L, G_WIDTH), 0.01),
        "gmlp_ln_b": nrm(ks[6], (L, G_WIDTH), 0.01),
        "gmlp_ws": nrm(ks[7], (L, G_HEADS, CHUNK, CHUNK), CHUNK ** -0.5),
        "gmlp_bs": 1.0 + nrm(ks[8], (L, G_HEADS, CHUNK), 0.01),
        "lambda_q1": nrm(ks[9], (L, DA_HEAD_DIM), 0.1),
        "lambda_k1": nrm(ks[10], (L, DA_HEAD_DIM), 0.1),
        "lambda_q2": nrm(ks[11], (L, DA_HEAD_DIM), 0.1),
        "lambda_k2": nrm(ks[12], (L, DA_HEAD_DIM), 0.1),
        "da_subln_g": 1.0 + nrm(ks[13], (L, DA_V_DIM), 0.01),
        "mem_ln_g": 1.0 + nrm(ks[14], (L, D), 0.01),
        "mem_ln_b": nrm(ks[15], (L, D), 0.01),
        "w_mem_kv": nrm(ks[16], (L, D, 2 * M_WIDTH), D ** -0.5),
        "w_out": nrm(ks[17], (L, MIX_WIDTH, D), BETA * MIX_WIDTH ** -0.5),
        "ln1_g": 1.0 + nrm(ks[18], (L, D), 0.01),
        "ln1_b": nrm(ks[19], (L, D), 0.01),
        "w_up": nrm(ks[20], (L, D, 2 * D_FF), D ** -0.5),
        "conv_w": nrm(ks[21], (L, CONV_W, 2 * D_FF), CONV_W ** -0.5),
        "conv_b": nrm(ks[22], (L, 2 * D_FF), 0.01),
        "w_down": nrm(ks[23], (L, D_FF, D), BETA * D_FF ** -0.5),
        "ln2_g": 1.0 + nrm(ks[24], (L, D), 0.01),
        "ln2_b": nrm(ks[25], (L, D), 0.01),
    }


def reference(x, mem, ln_emb_g, ln_emb_b, w_in, gmlp_ln_g, gmlp_ln_b, gmlp_ws, gmlp_bs,
              lambda_q1, lambda_k1, lambda_q2, lambda_k2, da_subln_g, mem_ln_g, mem_ln_b,
              w_mem_kv, w_out, ln1_g, ln1_b, w_up, conv_w, conv_b, w_down, ln2_g, ln2_b):
    B, T, _ = x.shape
    slopes = jnp.asarray(alibi_slopes(DA_HEADS))
    h = layer_norm(x, ln_emb_g, ln_emb_b)
    for l in range(DEPTH):
        lambda_init = 0.8 - 0.6 * math.exp(-0.3 * l)
        proj = h @ w_in[l]
        u, v, q, k, vd, qm = jnp.split(proj, SPLITS, axis=-1)
        y_g = gmlp_group(u, v, gmlp_ws[l], gmlp_bs[l], gmlp_ln_g[l], gmlp_ln_b[l])
        lam = (jnp.exp(jnp.sum(lambda_q1[l] * lambda_k1[l]).astype(jnp.float32))
               - jnp.exp(jnp.sum(lambda_q2[l] * lambda_k2[l]).astype(jnp.float32))
               + lambda_init)
        y_d = diff_attention(q.reshape(B, T, DA_HEADS, 2, DA_HEAD_DIM),
                             k.reshape(B, T, DA_HEADS, 2, DA_HEAD_DIM),
                             vd.reshape(B, T, DA_HEADS, DA_V_DIM), lam, slopes)
        y_d = (rms_norm(y_d, da_subln_g[l]) * (1.0 - lambda_init)).reshape(B, T, DA_WIDTH)
        kv = layer_norm(mem, mem_ln_g[l], mem_ln_b[l]) @ w_mem_kv[l]
        y_m = memory_attention(qm.reshape(B, T, M_HEADS, M_HEAD_DIM), kv)
        y = jnp.concatenate([y_g, y_d, y_m], axis=-1) @ w_out[l]
        h = layer_norm(ALPHA * h + y, ln1_g[l], ln1_b[l])
        f = conv_ffn(h, w_up[l], conv_w[l], conv_b[l], w_down[l])
        h = layer_norm(ALPHA * h + f, ln2_g[l], ln2_b[l])
    return h
```

```python
import functools
import math

import jax
import jax.numpy as jnp
from jax import lax
from jax.experimental import pallas as pl
from jax.experimental.pallas import tpu as pltpu

F32 = jnp.float32
BF16 = jnp.bfloat16

D_MODEL = 1024
DEPTH = 1
MEM_LEN = 256
CHUNK = 128
G_HEADS = 4
G_WIDTH = D_MODEL // 4
G_DIM = G_WIDTH // G_HEADS
DA_HEADS = 4
DA_WIDTH = D_MODEL // 2
DA_V_DIM = DA_WIDTH // DA_HEADS
DA_HEAD_DIM = DA_V_DIM // 2
DA_QK = DA_HEADS * 2 * DA_HEAD_DIM
M_HEADS = 4
M_WIDTH = D_MODEL // 4
M_HEAD_DIM = M_WIDTH // M_HEADS
D_FF = 2816
CONV_W = 3
LN_EPS = 1e-5
ALPHA = (2.0 * DEPTH) ** 0.25
LOG2E = math.log2(math.e)

V7X_VMEM_LIMIT_BYTES = 56 * 1024 * 1024
BF16_SUBLANE_TILE = 16

PROJ_ROWS = 512
ATT_Q = 512
ATT_K = 512
FFN_ROWS = 512
FFN_COLS = 256
HALO = BF16_SUBLANE_TILE

NT_DIMS = (((1,), (1,)), ((), ()))


def _layer_norm(x, g, b):
    mu = jnp.mean(x, axis=-1, keepdims=True)
    xc = x - mu
    var = jnp.mean(xc * xc, axis=-1, keepdims=True)
    return xc * lax.rsqrt(var + LN_EPS) * g + b


def _gelu(x):
    return 0.5 * x * (1.0 + lax.erf(x * (2.0 ** -0.5)))


def _dot(a, b):
    return jnp.dot(a, b, preferred_element_type=F32)


def _params(semantics):
    return pltpu.CompilerParams(dimension_semantics=semantics,
                                vmem_limit_bytes=V7X_VMEM_LIMIT_BYTES)


def _mem_kv_kernel(mem_ref, g_ref, b_ref, wkT_ref, wv_ref, kmT_ref, vm_ref):
    m = _layer_norm(mem_ref[...], g_ref[...], b_ref[...]).astype(BF16)
    kT = lax.dot_general(wkT_ref[...], m, NT_DIMS, preferred_element_type=F32)
    kT = kT * (M_HEAD_DIM ** -0.5)
    v = _dot(m, wv_ref[...])
    head_of_row = lax.broadcasted_iota(jnp.int32, (M_WIDTH, MEM_LEN), 0) // M_HEAD_DIM
    head_of_col = lax.broadcasted_iota(jnp.int32, (MEM_LEN, M_WIDTH), 1) // M_HEAD_DIM
    for h in range(M_HEADS):
        kmT_ref[h] = jnp.where(head_of_row == h, kT, 0.0).astype(BF16)
        vm_ref[h] = jnp.where(head_of_col == h, v, 0.0).astype(BF16)


def _mem_kv(mem, g, b, wkT, wv):
    B = mem.shape[0]
    row = lambda: pl.BlockSpec((1, D_MODEL), lambda i: (0, 0))
    return pl.pallas_call(
        _mem_kv_kernel,
        grid=(B,),
        in_specs=[
            pl.BlockSpec((None, MEM_LEN, D_MODEL), lambda i: (i, 0, 0)),
            row(), row(),
            pl.BlockSpec((M_WIDTH, D_MODEL), lambda i: (0, 0)),
            pl.BlockSpec((D_MODEL, M_WIDTH), lambda i: (0, 0)),
        ],
        out_specs=[
            pl.BlockSpec((None, M_HEADS, M_WIDTH, MEM_LEN), lambda i: (i, 0, 0, 0)),
            pl.BlockSpec((None, M_HEADS, MEM_LEN, M_WIDTH), lambda i: (i, 0, 0, 0)),
        ],
        out_shape=[
            jax.ShapeDtypeStruct((B, M_HEADS, M_WIDTH, MEM_LEN), BF16),
            jax.ShapeDtypeStruct((B, M_HEADS, MEM_LEN, M_WIDTH), BF16),
        ],
        compiler_params=_params(("arbitrary",)),
        name="mem_kv",
    )(mem, g, b, wkT, wv)


_A_U = 0
_A_V = G_WIDTH
_A_K = 2 * G_WIDTH
_A_QM = 2 * G_WIDTH + DA_QK
_A_END = _A_QM + M_WIDTH


def _in_proj_kernel(x_ref, eg_ref, eb_ref, wa_ref, wbT_ref, gg_ref, gb_ref, ws_ref, bs_ref,
                    kmT_ref, vm_ref,
                    h0_ref, yg_ref, ym_ref, qT_ref, k_ref, vT_ref):
    rows = x_ref.shape[0]
    h = _layer_norm(x_ref[...], eg_ref[...], eb_ref[...])
    h0_ref[...] = h
    hb = h.astype(BF16)

    pa = _dot(hb, wa_ref[...])
    pbT = lax.dot_general(wbT_ref[...], hb, NT_DIMS, preferred_element_type=F32)
    qT_ref[...] = (pbT[:DA_QK] * (DA_HEAD_DIM ** -0.5 * LOG2E)).astype(BF16)
    vT_ref[...] = pbT[DA_QK:].astype(BF16)
    k_ref[...] = pa[:, _A_K:_A_QM].astype(BF16)

    u = _gelu(pa[:, _A_U:_A_V])
    v = _layer_norm(_gelu(pa[:, _A_V:_A_K]), gg_ref[...], gb_ref[...]).astype(BF16)
    group_of_lane = lax.broadcasted_iota(jnp.int32, (CHUNK, G_WIDTH), 1) // G_DIM
    ws = ws_ref[...]
    bs = bs_ref[...]
    for c in range(rows // CHUNK):
        sl = slice(c * CHUNK, (c + 1) * CHUNK)
        r = _dot(ws, v[sl])
        s = r[(G_HEADS - 1) * CHUNK:]
        for g in range(G_HEADS - 2, -1, -1):
            s = jnp.where(group_of_lane == g, r[g * CHUNK:(g + 1) * CHUNK], s)
        yg_ref[sl, :] = (u[sl] * (s + bs)).astype(BF16)

    qm = pa[:, _A_QM:_A_END].astype(BF16)
    o = jnp.zeros((rows, M_WIDTH), F32)
    for hd in range(M_HEADS):
        s = _dot(qm, kmT_ref[hd])
        p = jnp.exp(s - jnp.max(s, axis=-1, keepdims=True))
        p = p / jnp.sum(p, axis=-1, keepdims=True)
        o = o + _dot(p.astype(BF16), vm_ref[hd])
    ym_ref[...] = o.astype(BF16)


def _in_proj(x, eg, eb, w_a, w_bT, gg, gb, ws, bs_tile, kmT, vm):
    B, T, D = x.shape
    R = PROJ_ROWS
    const2 = lambda shape: pl.BlockSpec(shape, lambda b, i: (0, 0))
    return pl.pallas_call(
        _in_proj_kernel,
        grid=(B, T // R),
        in_specs=[
            pl.BlockSpec((None, R, D), lambda b, i: (b, i, 0)),
            const2((1, D)), const2((1, D)),
            const2((D, _A_END)),
            const2((DA_QK + DA_WIDTH, D)),
            const2((1, G_WIDTH)), const2((1, G_WIDTH)),
            const2((G_HEADS * CHUNK, CHUNK)),
            const2((CHUNK, G_WIDTH)),
            pl.BlockSpec((None, M_HEADS, M_WIDTH, MEM_LEN), lambda b, i: (b, 0, 0, 0)),
            pl.BlockSpec((None, M_HEADS, MEM_LEN, M_WIDTH), lambda b, i: (b, 0, 0, 0)),
        ],
        out_specs=[
            pl.BlockSpec((None, R, D), lambda b, i: (b, i, 0)),
            pl.BlockSpec((None, R, G_WIDTH), lambda b, i: (b, i, 0)),
            pl.BlockSpec((None, R, M_WIDTH), lambda b, i: (b, i, 0)),
            pl.BlockSpec((None, DA_QK, R), lambda b, i: (b, 0, i)),
            pl.BlockSpec((None, R, DA_QK), lambda b, i: (b, i, 0)),
            pl.BlockSpec((None, DA_WIDTH, R), lambda b, i: (b, 0, i)),
        ],
        out_shape=[
            jax.ShapeDtypeStruct((B, T, D), F32),
            jax.ShapeDtypeStruct((B, T, G_WIDTH), BF16),
            jax.ShapeDtypeStruct((B, T, M_WIDTH), BF16),
            jax.ShapeDtypeStruct((B, DA_QK, T), BF16),
            jax.ShapeDtypeStruct((B, T, DA_QK), BF16),
            jax.ShapeDtypeStruct((B, DA_WIDTH, T), BF16),
        ],
        compiler_params=_params(("arbitrary", "arbitrary")),
        name="in_proj",
    )(x, eg, eb, w_a, w_bT, gg, gb, ws, bs_tile, kmT, vm)


def _diff_attn_kernel(slopes_ref, qT_ref, k_ref, vT_ref, lq1_ref, lk1_ref, lq2_ref, lk2_ref,
                      g_ref, o_ref, acc_ref, *, lambda_init):
    tq = qT_ref.shape[1]
    T = k_ref.shape[0]
    hd = pl.program_id(1)
    qi = pl.program_id(2)

    qt = qT_ref[...]
    row = lax.broadcasted_iota(jnp.int32, qt.shape, 0)
    zero = jnp.zeros_like(qt)
    q1 = jnp.where(row < DA_HEAD_DIM, qt, zero)
    q2 = jnp.where(row >= DA_HEAD_DIM, qt, zero)

    neg_c = -(slopes_ref[hd] * LOG2E)
    qpos = (qi * tq + lax.broadcasted_iota(jnp.int32, (1, tq), 1)).astype(F32)
    kpos0 = lax.broadcasted_iota(jnp.int32, (ATT_K, 1), 0).astype(F32)

    acc_ref[...] = jnp.zeros_like(acc_ref)
    stat0 = (jnp.full((1, tq), -1e30, F32), jnp.zeros((1, tq), F32))

    def step(j, carry):
        (m1, l1), (m2, l2) = carry
        start = pl.multiple_of(j * ATT_K, ATT_K)
        kb = k_ref[pl.ds(start, ATT_K), :]
        vb = vT_ref[:, pl.ds(start, ATT_K)]
        kpos = kpos0 + (j * ATT_K).astype(F32)
        bias = neg_c * jnp.abs(kpos - qpos)

        def one(q, m, l, idx):
            s = _dot(kb, q) + bias
            m_new = jnp.maximum(m, jnp.max(s, axis=0, keepdims=True))
            alpha = jnp.exp2(m - m_new)
            p = jnp.exp2(s - m_new)
            l_new = alpha * l + jnp.sum(p, axis=0, keepdims=True)
            acc_ref[idx] = alpha * acc_ref[idx] + _dot(vb, p.astype(BF16))
            return m_new, l_new

        return one(q1, m1, l1, 0), one(q2, m2, l2, 1)

    (m1, l1), (m2, l2) = lax.fori_loop(0, T // ATT_K, step, (stat0, stat0))

    lam = (jnp.exp(jnp.sum(lq1_ref[...] * lk1_ref[...], axis=-1, keepdims=True))
           - jnp.exp(jnp.sum(lq2_ref[...] * lk2_ref[...], axis=-1, keepdims=True))
           + lambda_init)
    oT = acc_ref[0] / l1 - lam * (acc_ref[1] / l2)
    o = oT.T
    ms = jnp.mean(o * o, axis=-1, keepdims=True)
    o_ref[...] = (o * lax.rsqrt(ms + LN_EPS) * g_ref[...] * (1.0 - lambda_init)).astype(o_ref.dtype)


def _diff_attn(slopes, qT, k, vT, lq1, lk1, lq2, lk2, g, lambda_init):
    B, _, T = qT.shape
    two_dk = 2 * DA_HEAD_DIM
    vec = lambda: pl.BlockSpec((1, DA_HEAD_DIM), lambda b, h, i: (0, 0))
    return pl.pallas_call(
        functools.partial(_diff_attn_kernel, lambda_init=lambda_init),
        grid=(B, DA_HEADS, T // ATT_Q),
        in_specs=[
            pl.BlockSpec(memory_space=pltpu.SMEM),
            pl.BlockSpec((None, two_dk, ATT_Q), lambda b, h, i: (b, h, i)),
            pl.BlockSpec((None, T, two_dk), lambda b, h, i: (b, 0, h)),
            pl.BlockSpec((None, DA_V_DIM, T), lambda b, h, i: (b, h, 0)),
            vec(), vec(), vec(), vec(),
            pl.BlockSpec((1, DA_V_DIM), lambda b, h, i: (0, 0)),
        ],
        out_specs=pl.BlockSpec((None, ATT_Q, DA_V_DIM), lambda b, h, i: (b, i, h)),
        out_shape=jax.ShapeDtypeStruct((B, T, DA_WIDTH), BF16),
        scratch_shapes=[pltpu.VMEM((2, DA_V_DIM, ATT_Q), F32)],
        compiler_params=_params(("arbitrary", "arbitrary", "arbitrary")),
        name="diff_attn",
    )(slopes, qT, k, vT, lq1, lk1, lq2, lk2, g)


def _out_proj_kernel(h0_ref, yg_ref, yd_ref, ym_ref, w_ref, g_ref, b_ref, h1_ref, h1b_ref):
    y = (_dot(yg_ref[...], w_ref[:G_WIDTH])
         + _dot(yd_ref[...], w_ref[G_WIDTH:G_WIDTH + DA_WIDTH])
         + _dot(ym_ref[...], w_ref[G_WIDTH + DA_WIDTH:]))
    h1 = _layer_norm(ALPHA * h0_ref[...] + y, g_ref[...], b_ref[...])
    h1_ref[...] = h1
    h1b_ref[...] = h1.astype(BF16)


def _out_proj(h0, yg, yd, ym, w_out, g, b):
    N, D = h0.shape
    R = PROJ_ROWS
    rows = lambda width: pl.BlockSpec((R, width), lambda i: (i, 0))
    const = lambda shape: pl.BlockSpec(shape, lambda i: (0, 0))
    return pl.pallas_call(
        _out_proj_kernel,
        grid=(N // R,),
        in_specs=[rows(D), rows(G_WIDTH), rows(DA_WIDTH), rows(M_WIDTH),
                  const((D, D)), const((1, D)), const((1, D))],
        out_specs=[rows(D), rows(D)],
        out_shape=[jax.ShapeDtypeStruct((N, D), F32), jax.ShapeDtypeStruct((N, D), BF16)],
        compiler_params=_params(("arbitrary",)),
        name="out_proj",
    )(h0, yg, yd, ym, w_out, g, b)


def _conv_ffn_kernel(h1_ref, hb_ref, prev_ref, next_ref, wup_ref, cw_ref, cb_ref, wdn_ref,
                     g_ref, b_ref, o_ref, *, steps_per_seq):
    rows = hb_ref.shape[0]
    i = pl.program_id(0) % steps_per_seq
    prev = jnp.where(i == 0, jnp.zeros_like(prev_ref[...]), prev_ref[...])
    nxt = jnp.where(i == steps_per_seq - 1, jnp.zeros_like(next_ref[...]), next_ref[...])
    hext = jnp.concatenate([prev, hb_ref[...], nxt], axis=0)

    def conv(cols):
        a = _dot(hext, wup_ref[:, cols])
        cw = cw_ref[:, cols]
        return (cw[0:1] * a[HALO - 1:HALO - 1 + rows]
                + cw[1:2] * a[HALO:HALO + rows]
                + cw[2:3] * a[HALO + 1:HALO + 1 + rows]
                + cb_ref[:, cols])

    f = jnp.zeros((rows, D_MODEL), F32)
    for c in range(D_FF // FFN_COLS):
        gate = conv(slice(c * FFN_COLS, (c + 1) * FFN_COLS))
        val = conv(slice(D_FF + c * FFN_COLS, D_FF + (c + 1) * FFN_COLS))
        act = (_gelu(gate) * val).astype(BF16)
        f = f + _dot(act, wdn_ref[c * FFN_COLS:(c + 1) * FFN_COLS, :])
    o_ref[...] = _layer_norm(ALPHA * h1_ref[...] + f, g_ref[...], b_ref[...])


def _conv_ffn(h1, h1b, w_up, conv_w, conv_b, w_down, g, b, seq_len):
    N, D = h1.shape
    R = FFN_ROWS
    halo_per_step = R // HALO
    n_halo = N // HALO
    const = lambda shape: pl.BlockSpec(shape, lambda i: (0, 0))
    return pl.pallas_call(
        functools.partial(_conv_ffn_kernel, steps_per_seq=seq_len // R),
        grid=(N // R,),
        in_specs=[
            pl.BlockSpec((R, D), lambda i: (i, 0)),
            pl.BlockSpec((R, D), lambda i: (i, 0)),
            pl.BlockSpec((HALO, D), lambda i: (jnp.maximum(i * halo_per_step - 1, 0), 0)),
            pl.BlockSpec((HALO, D), lambda i: (jnp.minimum((i + 1) * halo_per_step, n_halo - 1), 0)),
            const((D, 2 * D_FF)), const((CONV_W, 2 * D_FF)), const((1, 2 * D_FF)),
            const((D_FF, D)), const((1, D)), const((1, D)),
        ],
        out_specs=pl.BlockSpec((R, D), lambda i: (i, 0)),
        out_shape=jax.ShapeDtypeStruct((N, D), F32),
        compiler_params=_params(("arbitrary",)),
        name="conv_ffn",
    )(h1, h1b, h1b, h1b, w_up, conv_w, conv_b, w_down, g, b)


def _alibi_slopes(n):
    return jnp.asarray([2.0 ** (-8.0 * (i + 1) / n) for i in range(n)], F32)


def kernel(x, mem, ln_emb_g, ln_emb_b, w_in, gmlp_ln_g, gmlp_ln_b, gmlp_ws, gmlp_bs, lambda_q1, lambda_k1, lambda_q2, lambda_k2, da_subln_g, mem_ln_g, mem_ln_b, w_mem_kv, w_out, ln1_g, ln1_b, w_up, conv_w, conv_b, w_down, ln2_g, ln2_b):
    B, T, D = x.shape
    assert DEPTH == 1 and w_in.shape[0] == 1
    assert T % PROJ_ROWS == 0 and T % ATT_Q == 0 and T % ATT_K == 0 and T % FFN_ROWS == 0
    assert D_FF % FFN_COLS == 0
    l = 0
    lambda_init = 0.8 - 0.6 * math.exp(-0.3 * l)
    row = lambda a: a.reshape(1, -1)

    w = w_in[l]
    u_end, v_end = G_WIDTH, 2 * G_WIDTH
    q_end = v_end + DA_QK
    k_end = q_end + DA_QK
    vd_end = k_end + DA_WIDTH
    w_a = jnp.concatenate([w[:, :v_end], w[:, q_end:k_end], w[:, vd_end:]], axis=1).astype(BF16)
    w_bT = jnp.concatenate([w[:, v_end:q_end], w[:, k_end:vd_end]], axis=1).T.astype(BF16)
    w_kmT = w_mem_kv[l][:, :M_WIDTH].T.astype(BF16)
    w_vm = w_mem_kv[l][:, M_WIDTH:].astype(BF16)
    ws = gmlp_ws[l].reshape(G_HEADS * CHUNK, CHUNK).astype(BF16)
    bs_tile = jnp.repeat(gmlp_bs[l].T, G_DIM, axis=1)

    kmT, vm = _mem_kv(mem, row(mem_ln_g[l]), row(mem_ln_b[l]), w_kmT, w_vm)
    h0, yg, ym, qT, k, vT = _in_proj(
        x, row(ln_emb_g), row(ln_emb_b), w_a, w_bT, row(gmlp_ln_g[l]), row(gmlp_ln_b[l]),
        ws, bs_tile, kmT, vm)
    yd = _diff_attn(_alibi_slopes(DA_HEADS), qT, k, vT, row(lambda_q1[l]), row(lambda_k1[l]),
                    row(lambda_q2[l]), row(lambda_k2[l]), row(da_subln_g[l]), lambda_init)

    N = B * T
    h1, h1b = _out_proj(h0.reshape(N, D), yg.reshape(N, G_WIDTH), yd.reshape(N, DA_WIDTH),
                        ym.reshape(N, M_WIDTH), w_out[l].astype(BF16), row(ln1_g[l]), row(ln1_b[l]))
    out = _conv_ffn(h1, h1b, w_up[l].astype(BF16), conv_w[l], row(conv_b[l]),
                    w_down[l].astype(BF16), row(ln2_g[l]), row(ln2_b[l]), T)
    return out.reshape(B, T, D)
```

```python
import functools
import math

import jax
import jax.numpy as jnp
from jax import lax
from jax.experimental import pallas as pl
from jax.experimental.pallas import tpu as pltpu

F32 = jnp.float32
BF16 = jnp.bfloat16

D_MODEL = 1024
DEPTH = 1
MEM_LEN = 256
CHUNK = 128
G_HEADS = 4
G_WIDTH = D_MODEL // 4
G_DIM = G_WIDTH // G_HEADS
DA_HEADS = 4
DA_WIDTH = D_MODEL // 2
DA_V_DIM = DA_WIDTH // DA_HEADS
DA_HEAD_DIM = DA_V_DIM // 2
DA_QK = DA_HEADS * 2 * DA_HEAD_DIM
M_HEADS = 4
M_WIDTH = D_MODEL // 4
M_HEAD_DIM = M_WIDTH // M_HEADS
D_FF = 2816
CONV_W = 3
LN_EPS = 1e-5
ALPHA = (2.0 * DEPTH) ** 0.25
LOG2E = math.log2(math.e)

V7X_VMEM_LIMIT_BYTES = 56 * 1024 * 1024
BF16_SUBLANE_TILE = 16

PROJ_ROWS = 512
ATT_Q = 512
ATT_K = 512
FFN_ROWS = 512
FFN_COLS = 256
HALO = BF16_SUBLANE_TILE

NT_DIMS = (((1,), (1,)), ((), ()))


def _layer_norm(x, g, b):
    mu = jnp.mean(x, axis=-1, keepdims=True)
    xc = x - mu
    var = jnp.mean(xc * xc, axis=-1, keepdims=True)
    return xc * lax.rsqrt(var + LN_EPS) * g + b


def _gelu(x):
    return 0.5 * x * (1.0 + lax.erf(x * (2.0 ** -0.5)))


def _dot(a, b):
    return jnp.dot(a, b, preferred_element_type=F32)


def _params(semantics):
    return pltpu.CompilerParams(dimension_semantics=semantics,
                                vmem_limit_bytes=V7X_VMEM_LIMIT_BYTES)


def _mem_kv_kernel(mem_ref, g_ref, b_ref, wkT_ref, wv_ref, kmT_ref, vm_ref):
    m = _layer_norm(mem_ref[...], g_ref[...], b_ref[...]).astype(BF16)
    kT = lax.dot_general(wkT_ref[...], m, NT_DIMS, preferred_element_type=F32)
    kT = kT * (M_HEAD_DIM ** -0.5)
    v = _dot(m, wv_ref[...])
    head_of_row = lax.broadcasted_iota(jnp.int32, (M_WIDTH, MEM_LEN), 0) // M_HEAD_DIM
    head_of_col = lax.broadcasted_iota(jnp.int32, (MEM_LEN, M_WIDTH), 1) // M_HEAD_DIM
    for h in range(M_HEADS):
        kmT_ref[h] = jnp.where(head_of_row == h, kT, 0.0).astype(BF16)
        vm_ref[h] = jnp.where(head_of_col == h, v, 0.0).astype(BF16)


def _mem_kv(mem, g, b, wkT, wv):
    B = mem.shape[0]
    row = lambda: pl.BlockSpec((1, D_MODEL), lambda i: (0, 0))
    return pl.pallas_call(
        _mem_kv_kernel,
        grid=(B,),
        in_specs=[
            pl.BlockSpec((None, MEM_LEN, D_MODEL), lambda i: (i, 0, 0)),
            row(), row(),
            pl.BlockSpec((M_WIDTH, D_MODEL), lambda i: (0, 0)),
            pl.BlockSpec((D_MODEL, M_WIDTH), lambda i: (0, 0)),
        ],
        out_specs=[
            pl.BlockSpec((None, M_HEADS, M_WIDTH, MEM_LEN), lambda i: (i, 0, 0, 0)),
            pl.BlockSpec((None, M_HEADS, MEM_LEN, M_WIDTH), lambda i: (i, 0, 0, 0)),
        ],
        out_shape=[
            jax.ShapeDtypeStruct((B, M_HEADS, M_WIDTH, MEM_LEN), BF16),
            jax.ShapeDtypeStruct((B, M_HEADS, MEM_LEN, M_WIDTH), BF16),
        ],
        compiler_params=_params(("arbitrary",)),
        name="mem_kv",
    )(mem, g, b, wkT, wv)


_A_U = 0
_A_V = G_WIDTH
_A_K = 2 * G_WIDTH
_A_QM = 2 * G_WIDTH + DA_QK
_A_END = _A_QM + M_WIDTH


def _in_proj_kernel(x_ref, eg_ref, eb_ref, wa_ref, wbT_ref, gg_ref, gb_ref, ws_ref, bs_ref,
                    kmT_ref, vm_ref,
                    h0_ref, yg_ref, ym_ref, qT_ref, k_ref, vT_ref):
    rows = x_ref.shape[0]
    h = _layer_norm(x_ref[...], eg_ref[...], eb_ref[...])
    h0_ref[...] = h
    hb = h.astype(BF16)

    pa = _dot(hb, wa_ref[...])
    pbT = lax.dot_general(wbT_ref[...], hb, NT_DIMS, preferred_element_type=F32)
    qT_ref[...] = (pbT[:DA_QK] * (DA_HEAD_DIM ** -0.5 * LOG2E)).astype(BF16)
    vT_ref[...] = pbT[DA_QK:].astype(BF16)
    k_ref[...] = pa[:, _A_K:_A_QM].astype(BF16)

    u = _gelu(pa[:, _A_U:_A_V])
    v = _layer_norm(_gelu(pa[:, _A_V:_A_K]), gg_ref[...], gb_ref[...]).astype(BF16)
    group_of_lane = lax.broadcasted_iota(jnp.int32, (CHUNK, G_WIDTH), 1) // G_DIM
    ws = ws_ref[...]
    bs = bs_ref[...]
    for c in range(rows // CHUNK):
        sl = slice(c * CHUNK, (c + 1) * CHUNK)
        r = _dot(ws, v[sl])
        s = r[(G_HEADS - 1) * CHUNK:]
        for g in range(G_HEADS - 2, -1, -1):
            s = jnp.where(group_of_lane == g, r[g * CHUNK:(g + 1) * CHUNK], s)
        yg_ref[sl, :] = (u[sl] * (s + bs)).astype(BF16)

    qm = pa[:, _A_QM:_A_END].astype(BF16)
    o = jnp.zeros((rows, M_WIDTH), F32)
    for hd in range(M_HEADS):
        s = _dot(qm, kmT_ref[hd])
        p = jnp.exp(s - jnp.max(s, axis=-1, keepdims=True))
        p = p / jnp.sum(p, axis=-1, keepdims=True)
        o = o + _dot(p.astype(BF16), vm_ref[hd])
    ym_ref[...] = o.astype(BF16)


def _in_proj(x, eg, eb, w_a, w_bT, gg, gb, ws, bs_tile, kmT, vm):
    B, T, D = x.shape
    R = PROJ_ROWS
    const2 = lambda shape: pl.BlockSpec(shape, lambda b, i: (0, 0))
    return pl.pallas_call(
        _in_proj_kernel,
        grid=(B, T // R),
        in_specs=[
            pl.BlockSpec((None, R, D), lambda b, i: (b, i, 0)),
            const2((1, D)), const2((1, D)),
            const2((D, _A_END)),
            const2((DA_QK + DA_WIDTH, D)),
            const2((1, G_WIDTH)), const2((1, G_WIDTH)),
            const2((G_HEADS * CHUNK, CHUNK)),
            const2((CHUNK, G_WIDTH)),
            pl.BlockSpec((None, M_HEADS, M_WIDTH, MEM_LEN), lambda b, i: (b, 0, 0, 0)),
            pl.BlockSpec((None, M_HEADS, MEM_LEN, M_WIDTH), lambda b, i: (b, 0, 0, 0)),
        ],
        out_specs=[
            pl.BlockSpec((None, R, D), lambda b, i: (b, i, 0)),
            pl.BlockSpec((None, R, G_WIDTH), lambda b, i: (b, i, 0)),
            pl.BlockSpec((None, R, M_WIDTH), lambda b, i: (b, i, 0)),
            pl.BlockSpec((None, DA_QK, R), lambda b, i: (b, 0, i)),
            pl.BlockSpec((None, R, DA_QK), lambda b, i: (b, i, 0)),
            pl.BlockSpec((None, DA_WIDTH, R), lambda b, i: (b, 0, i)),
        ],
        out_shape=[
            jax.ShapeDtypeStruct((B, T, D), F32),
            jax.ShapeDtypeStruct((B, T, G_WIDTH), BF16),
            jax.ShapeDtypeStruct((B, T, M_WIDTH), BF16),
            jax.ShapeDtypeStruct((B, DA_QK, T), BF16),
            jax.ShapeDtypeStruct((B, T, DA_QK), BF16),
            jax.ShapeDtypeStruct((B, DA_WIDTH, T), BF16),
        ],
        compiler_params=_params(("arbitrary", "arbitrary")),
        name="in_proj",
    )(x, eg, eb, w_a, w_bT, gg, gb, ws, bs_tile, kmT, vm)


_POS_SPLIT = 64
_BIAS_ROWS = BF16_SUBLANE_TILE
_ACC_ROWS = DA_V_DIM + BF16_SUBLANE_TILE


def _bf16_split3(x):
    a = x.astype(BF16).astype(F32)
    b = (x - a).astype(BF16).astype(F32)
    c = x - a - b
    return a, b, c


def _diff_attn_kernel(slopes_ref, qT_ref, k_ref, pos_ref, vT_ref, lq1_ref, lk1_ref, lq2_ref,
                      lk2_ref, g_ref, o_ref, qa_ref, base_ref, sa_ref, sb_ref, bma_ref, bmb_ref,
                      m_ref, acc_ref, diag_ref, *, lambda_init):
    tq = ATT_Q
    tk = ATT_K
    two_dk = 2 * DA_HEAD_DIM
    nk = k_ref.shape[0] // tk
    hd = pl.program_id(1)
    qi = pl.program_id(2)
    c = slopes_ref[hd] * LOG2E

    @pl.when(qi == 0)
    def _():
        kk = lax.broadcasted_iota(jnp.int32, (tk, tq), 0)
        qq = lax.broadcasted_iota(jnp.int32, (tk, tq), 1)
        diag_ref[...] = -c * jnp.abs(kk - qq).astype(F32)

    qt = qT_ref[...]
    zero = jnp.zeros((DA_HEAD_DIM, tq), BF16)
    qa_ref[:DA_HEAD_DIM, :] = jnp.concatenate([qt[:DA_HEAD_DIM], zero], axis=1)
    qa_ref[DA_HEAD_DIM:two_dk, :] = jnp.concatenate([zero, qt[DA_HEAD_DIM:]], axis=1)
    qa_ref[two_dk + _BIAS_ROWS:, :] = jnp.zeros((two_dk - _BIAS_ROWS, 2 * tq), BF16)

    lane = lax.broadcasted_iota(jnp.int32, (1, 2 * tq), 1)
    qpos = (qi * tq + jnp.where(lane >= tq, lane - tq, lane)).astype(F32)
    c1, c2, c3 = _bf16_split3(jnp.full((1, 2 * tq), c, F32))
    v1, v2, v3 = _bf16_split3(c * qpos)
    r = lax.broadcasted_iota(jnp.int32, (_BIAS_ROWS, 2 * tq), 0)
    base = jnp.zeros((_BIAS_ROWS, 2 * tq), F32)
    for idx, val in enumerate((-c1, -c2, -c3, -c1, -c2, -c3, v1, v2, v3)):
        base = jnp.where(r == idx, val, base)
    base_ref[...] = base

    ones_row = jnp.where(lax.broadcasted_iota(jnp.int32, (BF16_SUBLANE_TILE, tk), 0) == 0,
                         1.0, 0.0).astype(BF16)

    def scores(jn, s_ref, bm_ref):
        sign = jnp.where(jn > qi, 1.0, jnp.where(jn < qi, -1.0, 0.0)).astype(F32)
        qa_ref[two_dk:two_dk + _BIAS_ROWS, :] = (sign * base_ref[...]).astype(BF16)
        start = pl.multiple_of(jn * tk, tk)
        kb = jnp.concatenate([k_ref[pl.ds(start, tk), :], pos_ref[pl.ds(start, tk), :]], axis=1)
        s = _dot(kb, qa_ref[...])
        s_ref[...] = s
        bm_ref[...] = jnp.max(s, axis=0, keepdims=True)

    def fix_diagonal(jn, s_ref, bm_ref):
        @pl.when(jn == qi)
        def _():
            d = diag_ref[...]
            s = s_ref[...] + jnp.concatenate([d, d], axis=1)
            s_ref[...] = s
            bm_ref[...] = jnp.max(s, axis=0, keepdims=True)

    def consume(j, s_ref, bm_ref):
        m_old = m_ref[...]
        m_new = jnp.maximum(m_old, bm_ref[...])
        alpha = jnp.exp2(m_old - m_new)
        m_ref[...] = m_new
        p = jnp.exp2(s_ref[...] - m_new).astype(BF16)
        start = pl.multiple_of(j * tk, tk)
        vb = jnp.concatenate([vT_ref[:, pl.ds(start, tk)], ones_row], axis=0)
        acc_ref[...] = alpha * acc_ref[...] + _dot(vb, p)

    acc_ref[...] = jnp.zeros_like(acc_ref)
    m_ref[...] = jnp.full(m_ref.shape, -1e30, F32)
    scores(0, sa_ref, bma_ref)
    fix_diagonal(0, sa_ref, bma_ref)

    def pair(jj, carry):
        j = 2 * jj
        scores(j + 1, sb_ref, bmb_ref)
        consume(j, sa_ref, bma_ref)
        fix_diagonal(j + 1, sb_ref, bmb_ref)
        scores(j + 2, sa_ref, bma_ref)
        consume(j + 1, sb_ref, bmb_ref)
        fix_diagonal(j + 2, sa_ref, bma_ref)
        return carry

    lax.fori_loop(0, (nk - 2) // 2, pair, 0)
    scores(nk - 1, sb_ref, bmb_ref)
    consume(nk - 2, sa_ref, bma_ref)
    fix_diagonal(nk - 1, sb_ref, bmb_ref)
    consume(nk - 1, sb_ref, bmb_ref)

    lam = (jnp.exp(jnp.sum(lq1_ref[...] * lk1_ref[...], axis=-1, keepdims=True))
           - jnp.exp(jnp.sum(lq2_ref[...] * lk2_ref[...], axis=-1, keepdims=True))
           + lambda_init)
    inv_l = 1.0 / acc_ref[DA_V_DIM:DA_V_DIM + 1, :]
    oT = (acc_ref[:DA_V_DIM, :tq] * inv_l[:, :tq]
          - lam * (acc_ref[:DA_V_DIM, tq:] * inv_l[:, tq:]))
    o = oT.T
    ms = jnp.mean(o * o, axis=-1, keepdims=True)
    o_ref[...] = (o * lax.rsqrt(ms + LN_EPS) * g_ref[...] * (1.0 - lambda_init)).astype(o_ref.dtype)


def _key_positions(T):
    t = jnp.arange(T, dtype=jnp.int32)
    hi = ((t // _POS_SPLIT) * _POS_SPLIT).astype(F32)
    lo = (t % _POS_SPLIT).astype(F32)
    one = jnp.ones((T,), F32)
    cols = jnp.stack([hi, hi, hi, lo, lo, lo, one, one, one], axis=1)
    return jnp.pad(cols, ((0, 0), (0, 2 * DA_HEAD_DIM - cols.shape[1]))).astype(BF16)


def _diff_attn(slopes, qT, k, vT, lq1, lk1, lq2, lk2, g, lambda_init):
    B, _, T = qT.shape
    two_dk = 2 * DA_HEAD_DIM
    assert ATT_Q == ATT_K and (T // ATT_K) % 2 == 0 and T // ATT_K >= 2
    vec = lambda: pl.BlockSpec((1, DA_HEAD_DIM), lambda b, h, i: (0, 0))
    return pl.pallas_call(
        functools.partial(_diff_attn_kernel, lambda_init=lambda_init),
        grid=(B, DA_HEADS, T // ATT_Q),
        in_specs=[
            pl.BlockSpec(memory_space=pltpu.SMEM),
            pl.BlockSpec((None, two_dk, ATT_Q), lambda b, h, i: (b, h, i)),
            pl.BlockSpec((None, T, two_dk), lambda b, h, i: (b, 0, h)),
            pl.BlockSpec((T, two_dk), lambda b, h, i: (0, 0)),
            pl.BlockSpec((None, DA_V_DIM, T), lambda b, h, i: (b, h, 0)),
            vec(), vec(), vec(), vec(),
            pl.BlockSpec((1, DA_V_DIM), lambda b, h, i: (0, 0)),
        ],
        out_specs=pl.BlockSpec((None, ATT_Q, DA_V_DIM), lambda b, h, i: (b, i, h)),
        out_shape=jax.ShapeDtypeStruct((B, T, DA_WIDTH), BF16),
        scratch_shapes=[
            pltpu.VMEM((2 * two_dk, 2 * ATT_Q), BF16),
            pltpu.VMEM((_BIAS_ROWS, 2 * ATT_Q), F32),
            pltpu.VMEM((ATT_K, 2 * ATT_Q), F32),
            pltpu.VMEM((ATT_K, 2 * ATT_Q), F32),
            pltpu.VMEM((1, 2 * ATT_Q), F32),
            pltpu.VMEM((1, 2 * ATT_Q), F32),
            pltpu.VMEM((1, 2 * ATT_Q), F32),
            pltpu.VMEM((_ACC_ROWS, 2 * ATT_Q), F32),
            pltpu.VMEM((ATT_K, ATT_Q), F32),
        ],
        compiler_params=_params(("arbitrary", "arbitrary", "arbitrary")),
        name="diff_attn",
    )(slopes, qT, k, _key_positions(T), vT, lq1, lk1, lq2, lk2, g)


def _out_proj_kernel(h0_ref, yg_ref, yd_ref, ym_ref, w_ref, g_ref, b_ref, h1_ref, h1b_ref):
    y = (_dot(yg_ref[...], w_ref[:G_WIDTH])
         + _dot(yd_ref[...], w_ref[G_WIDTH:G_WIDTH + DA_WIDTH])
         + _dot(ym_ref[...], w_ref[G_WIDTH + DA_WIDTH:]))
    h1 = _layer_norm(ALPHA * h0_ref[...] + y, g_ref[...], b_ref[...])
    h1_ref[...] = h1
    h1b_ref[...] = h1.astype(BF16)


def _out_proj(h0, yg, yd, ym, w_out, g, b):
    N, D = h0.shape
    R = PROJ_ROWS
    rows = lambda width: pl.BlockSpec((R, width), lambda i: (i, 0))
    const = lambda shape: pl.BlockSpec(shape, lambda i: (0, 0))
    return pl.pallas_call(
        _out_proj_kernel,
        grid=(N // R,),
        in_specs=[rows(D), rows(G_WIDTH), rows(DA_WIDTH), rows(M_WIDTH),
                  const((D, D)), const((1, D)), const((1, D))],
        out_specs=[rows(D), rows(D)],
        out_shape=[jax.ShapeDtypeStruct((N, D), F32), jax.ShapeDtypeStruct((N, D), BF16)],
        compiler_params=_params(("arbitrary",)),
        name="out_proj",
    )(h0, yg, yd, ym, w_out, g, b)


def _conv_ffn_kernel(h1_ref, hb_ref, prev_ref, next_ref, wup_ref, cw_ref, cb_ref, wdn_ref,
                     g_ref, b_ref, o_ref, *, steps_per_seq):
    rows = hb_ref.shape[0]
    i = pl.program_id(0) % steps_per_seq
    prev = jnp.where(i == 0, jnp.zeros_like(prev_ref[...]), prev_ref[...])
    nxt = jnp.where(i == steps_per_seq - 1, jnp.zeros_like(next_ref[...]), next_ref[...])
    hext = jnp.concatenate([prev, hb_ref[...], nxt], axis=0)

    def conv(cols):
        a = _dot(hext, wup_ref[:, cols])
        cw = cw_ref[:, cols]
        return (cw[0:1] * a[HALO - 1:HALO - 1 + rows]
                + cw[1:2] * a[HALO:HALO + rows]
                + cw[2:3] * a[HALO + 1:HALO + 1 + rows]
                + cb_ref[:, cols])

    f = jnp.zeros((rows, D_MODEL), F32)
    for c in range(D_FF // FFN_COLS):
        gate = conv(slice(c * FFN_COLS, (c + 1) * FFN_COLS))
        val = conv(slice(D_FF + c * FFN_COLS, D_FF + (c + 1) * FFN_COLS))
        act = (_gelu(gate) * val).astype(BF16)
        f = f + _dot(act, wdn_ref[c * FFN_COLS:(c + 1) * FFN_COLS, :])
    o_ref[...] = _layer_norm(ALPHA * h1_ref[...] + f, g_ref[...], b_ref[...])


def _conv_ffn(h1, h1b, w_up, conv_w, conv_b, w_down, g, b, seq_len):
    N, D = h1.shape
    R = FFN_ROWS
    halo_per_step = R // HALO
    n_halo = N // HALO
    const = lambda shape: pl.BlockSpec(shape, lambda i: (0, 0))
    return pl.pallas_call(
        functools.partial(_conv_ffn_kernel, steps_per_seq=seq_len // R),
        grid=(N // R,),
        in_specs=[
            pl.BlockSpec((R, D), lambda i: (i, 0)),
            pl.BlockSpec((R, D), lambda i: (i, 0)),
            pl.BlockSpec((HALO, D), lambda i: (jnp.maximum(i * halo_per_step - 1, 0), 0)),
            pl.BlockSpec((HALO, D), lambda i: (jnp.minimum((i + 1) * halo_per_step, n_halo - 1), 0)),
            const((D, 2 * D_FF)), const((CONV_W, 2 * D_FF)), const((1, 2 * D_FF)),
            const((D_FF, D)), const((1, D)), const((1, D)),
        ],
        out_specs=pl.BlockSpec((R, D), lambda i: (i, 0)),
        out_shape=jax.ShapeDtypeStruct((N, D), F32),
        compiler_params=_params(("arbitrary",)),
        name="conv_ffn",
    )(h1, h1b, h1b, h1b, w_up, conv_w, conv_b, w_down, g, b)


def _alibi_slopes(n):
    return jnp.asarray([2.0 ** (-8.0 * (i + 1) / n) for i in range(n)], F32)


def kernel(x, mem, ln_emb_g, ln_emb_b, w_in, gmlp_ln_g, gmlp_ln_b, gmlp_ws, gmlp_bs, lambda_q1, lambda_k1, lambda_q2, lambda_k2, da_subln_g, mem_ln_g, mem_ln_b, w_mem_kv, w_out, ln1_g, ln1_b, w_up, conv_w, conv_b, w_down, ln2_g, ln2_b):
    B, T, D = x.shape
    assert DEPTH == 1 and w_in.shape[0] == 1
    assert T % PROJ_ROWS == 0 and T % ATT_Q == 0 and T % ATT_K == 0 and T % FFN_ROWS == 0
    assert D_FF % FFN_COLS == 0
    l = 0
    lambda_init = 0.8 - 0.6 * math.exp(-0.3 * l)
    row = lambda a: a.reshape(1, -1)

    w = w_in[l]
    u_end, v_end = G_WIDTH, 2 * G_WIDTH
    q_end = v_end + DA_QK
    k_end = q_end + DA_QK
    vd_end = k_end + DA_WIDTH
    w_a = jnp.concatenate([w[:, :v_end], w[:, q_end:k_end], w[:, vd_end:]], axis=1).astype(BF16)
    w_bT = jnp.concatenate([w[:, v_end:q_end], w[:, k_end:vd_end]], axis=1).T.astype(BF16)
    w_kmT = w_mem_kv[l][:, :M_WIDTH].T.astype(BF16)
    w_vm = w_mem_kv[l][:, M_WIDTH:].astype(BF16)
    ws = gmlp_ws[l].reshape(G_HEADS * CHUNK, CHUNK).astype(BF16)
    bs_tile = jnp.repeat(gmlp_bs[l].T, G_DIM, axis=1)

    kmT, vm = _mem_kv(mem, row(mem_ln_g[l]), row(mem_ln_b[l]), w_kmT, w_vm)
    h0, yg, ym, qT, k, vT = _in_proj(
        x, row(ln_emb_g), row(ln_emb_b), w_a, w_bT, row(gmlp_ln_g[l]), row(gmlp_ln_b[l]),
        ws, bs_tile, kmT, vm)
    yd = _diff_attn(_alibi_slopes(DA_HEADS), qT, k, vT, row(lambda_q1[l]), row(lambda_k1[l]),
                    row(lambda_q2[l]), row(lambda_k2[l]), row(da_subln_g[l]), lambda_init)

    N = B * T
    h1, h1b = _out_proj(h0.reshape(N, D), yg.reshape(N, G_WIDTH), yd.reshape(N, DA_WIDTH),
                        ym.reshape(N, M_WIDTH), w_out[l].astype(BF16), row(ln1_g[l]), row(ln1_b[l]))
    out = _conv_ffn(h1, h1b, w_up[l].astype(BF16), conv_w[l], row(conv_b[l]),
                    w_down[l].astype(BF16), row(ln2_g[l]), row(ln2_b[l]), T)
    return out.reshape(B, T, D)
```

```python
import functools
import math

import jax
import jax.numpy as jnp
from jax import lax
from jax.experimental import pallas as pl
from jax.experimental.pallas import tpu as pltpu

F32 = jnp.float32
BF16 = jnp.bfloat16

D_MODEL = 1024
DEPTH = 1
MEM_LEN = 256
CHUNK = 128
G_HEADS = 4
G_WIDTH = D_MODEL // 4
G_DIM = G_WIDTH // G_HEADS
DA_HEADS = 4
DA_WIDTH = D_MODEL // 2
DA_V_DIM = DA_WIDTH // DA_HEADS
DA_HEAD_DIM = DA_V_DIM // 2
DA_QK = DA_HEADS * 2 * DA_HEAD_DIM
M_HEADS = 4
M_WIDTH = D_MODEL // 4
M_HEAD_DIM = M_WIDTH // M_HEADS
D_FF = 2816
CONV_W = 3
LN_EPS = 1e-5
ALPHA = (2.0 * DEPTH) ** 0.25
LOG2E = math.log2(math.e)

V7X_VMEM_LIMIT_BYTES = 56 * 1024 * 1024
BF16_SUBLANE_TILE = 16

PROJ_ROWS = 512
ATT_Q = 512
ATT_K = 512
FFN_ROWS = 512
FFN_COLS = 256
HALO = BF16_SUBLANE_TILE

NT_DIMS = (((1,), (1,)), ((), ()))


def _layer_norm(x, g, b):
    mu = jnp.mean(x, axis=-1, keepdims=True)
    xc = x - mu
    var = jnp.mean(xc * xc, axis=-1, keepdims=True)
    return xc * lax.rsqrt(var + LN_EPS) * g + b


def _gelu(x):
    return 0.5 * x * (1.0 + lax.erf(x * (2.0 ** -0.5)))


def _dot(a, b):
    return jnp.dot(a, b, preferred_element_type=F32)


def _params(semantics):
    return pltpu.CompilerParams(dimension_semantics=semantics,
                                vmem_limit_bytes=V7X_VMEM_LIMIT_BYTES)


def _mem_kv_kernel(mem_ref, g_ref, b_ref, wkT_ref, wv_ref, kmT_ref, vm_ref):
    m = _layer_norm(mem_ref[...], g_ref[...], b_ref[...]).astype(BF16)
    kT = lax.dot_general(wkT_ref[...], m, NT_DIMS, preferred_element_type=F32)
    kT = kT * (M_HEAD_DIM ** -0.5)
    v = _dot(m, wv_ref[...])
    head_of_row = lax.broadcasted_iota(jnp.int32, (M_WIDTH, MEM_LEN), 0) // M_HEAD_DIM
    head_of_col = lax.broadcasted_iota(jnp.int32, (MEM_LEN, M_WIDTH), 1) // M_HEAD_DIM
    for h in range(M_HEADS):
        kmT_ref[h] = jnp.where(head_of_row == h, kT, 0.0).astype(BF16)
        vm_ref[h] = jnp.where(head_of_col == h, v, 0.0).astype(BF16)


def _mem_kv(mem, g, b, wkT, wv):
    B = mem.shape[0]
    row = lambda: pl.BlockSpec((1, D_MODEL), lambda i: (0, 0))
    return pl.pallas_call(
        _mem_kv_kernel,
        grid=(B,),
        in_specs=[
            pl.BlockSpec((None, MEM_LEN, D_MODEL), lambda i: (i, 0, 0)),
            row(), row(),
            pl.BlockSpec((M_WIDTH, D_MODEL), lambda i: (0, 0)),
            pl.BlockSpec((D_MODEL, M_WIDTH), lambda i: (0, 0)),
        ],
        out_specs=[
            pl.BlockSpec((None, M_HEADS, M_WIDTH, MEM_LEN), lambda i: (i, 0, 0, 0)),
            pl.BlockSpec((None, M_HEADS, MEM_LEN, M_WIDTH), lambda i: (i, 0, 0, 0)),
        ],
        out_shape=[
            jax.ShapeDtypeStruct((B, M_HEADS, M_WIDTH, MEM_LEN), BF16),
            jax.ShapeDtypeStruct((B, M_HEADS, MEM_LEN, M_WIDTH), BF16),
        ],
        compiler_params=_params(("arbitrary",)),
        name="mem_kv",
    )(mem, g, b, wkT, wv)


_A_U = 0
_A_V = G_WIDTH
_A_K = 2 * G_WIDTH
_A_QM = 2 * G_WIDTH + DA_QK
_A_END = _A_QM + M_WIDTH


def _in_proj_kernel(x_ref, eg_ref, eb_ref, wa_ref, wbT_ref, gg_ref, gb_ref, ws_ref, bs_ref,
                    kmT_ref, vm_ref,
                    h0_ref, yg_ref, ym_ref, qT_ref, k_ref, vT_ref):
    rows = x_ref.shape[0]
    h = _layer_norm(x_ref[...], eg_ref[...], eb_ref[...])
    h0_ref[...] = h
    hb = h.astype(BF16)

    pa = _dot(hb, wa_ref[...])
    pbT = lax.dot_general(wbT_ref[...], hb, NT_DIMS, preferred_element_type=F32)
    qT_ref[...] = (pbT[:DA_QK] * (DA_HEAD_DIM ** -0.5 * LOG2E)).astype(BF16)
    vT_ref[...] = pbT[DA_QK:].astype(BF16)
    k_ref[...] = pa[:, _A_K:_A_QM].astype(BF16)

    u = _gelu(pa[:, _A_U:_A_V])
    v = _layer_norm(_gelu(pa[:, _A_V:_A_K]), gg_ref[...], gb_ref[...]).astype(BF16)
    group_of_lane = lax.broadcasted_iota(jnp.int32, (CHUNK, G_WIDTH), 1) // G_DIM
    ws = ws_ref[...]
    bs = bs_ref[...]
    for c in range(rows // CHUNK):
        sl = slice(c * CHUNK, (c + 1) * CHUNK)
        r = _dot(ws, v[sl])
        s = r[(G_HEADS - 1) * CHUNK:]
        for g in range(G_HEADS - 2, -1, -1):
            s = jnp.where(group_of_lane == g, r[g * CHUNK:(g + 1) * CHUNK], s)
        yg_ref[sl, :] = (u[sl] * (s + bs)).astype(BF16)

    qm = pa[:, _A_QM:_A_END].astype(BF16)
    o = jnp.zeros((rows, M_WIDTH), F32)
    for hd in range(M_HEADS):
        s = _dot(qm, kmT_ref[hd])
        p = jnp.exp(s - jnp.max(s, axis=-1, keepdims=True))
        p = p / jnp.sum(p, axis=-1, keepdims=True)
        o = o + _dot(p.astype(BF16), vm_ref[hd])
    ym_ref[...] = o.astype(BF16)


def _in_proj(x, eg, eb, w_a, w_bT, gg, gb, ws, bs_tile, kmT, vm):
    B, T, D = x.shape
    R = PROJ_ROWS
    const2 = lambda shape: pl.BlockSpec(shape, lambda b, i: (0, 0))
    return pl.pallas_call(
        _in_proj_kernel,
        grid=(B, T // R),
        in_specs=[
            pl.BlockSpec((None, R, D), lambda b, i: (b, i, 0)),
            const2((1, D)), const2((1, D)),
            const2((D, _A_END)),
            const2((DA_QK + DA_WIDTH, D)),
            const2((1, G_WIDTH)), const2((1, G_WIDTH)),
            const2((G_HEADS * CHUNK, CHUNK)),
            const2((CHUNK, G_WIDTH)),
            pl.BlockSpec((None, M_HEADS, M_WIDTH, MEM_LEN), lambda b, i: (b, 0, 0, 0)),
            pl.BlockSpec((None, M_HEADS, MEM_LEN, M_WIDTH), lambda b, i: (b, 0, 0, 0)),
        ],
        out_specs=[
            pl.BlockSpec((None, R, D), lambda b, i: (b, i, 0)),
            pl.BlockSpec((None, R, G_WIDTH), lambda b, i: (b, i, 0)),
            pl.BlockSpec((None, R, M_WIDTH), lambda b, i: (b, i, 0)),
            pl.BlockSpec((None, DA_QK, R), lambda b, i: (b, 0, i)),
            pl.BlockSpec((None, R, DA_QK), lambda b, i: (b, i, 0)),
            pl.BlockSpec((None, DA_WIDTH, R), lambda b, i: (b, 0, i)),
        ],
        out_shape=[
            jax.ShapeDtypeStruct((B, T, D), F32),
            jax.ShapeDtypeStruct((B, T, G_WIDTH), BF16),
            jax.ShapeDtypeStruct((B, T, M_WIDTH), BF16),
            jax.ShapeDtypeStruct((B, DA_QK, T), BF16),
            jax.ShapeDtypeStruct((B, T, DA_QK), BF16),
            jax.ShapeDtypeStruct((B, DA_WIDTH, T), BF16),
        ],
        compiler_params=_params(("arbitrary", "arbitrary")),
        name="in_proj",
    )(x, eg, eb, w_a, w_bT, gg, gb, ws, bs_tile, kmT, vm)


_POS_SPLIT = 64
_BIAS_ROWS = BF16_SUBLANE_TILE
_ACC_ROWS = DA_V_DIM + BF16_SUBLANE_TILE


def _bf16_split3(x):
    a = x.astype(BF16).astype(F32)
    b = (x - a).astype(BF16).astype(F32)
    c = x - a - b
    return a, b, c


def _diff_attn_kernel(slopes_ref, qT_ref, k_ref, pos_ref, vT_ref, lq1_ref, lk1_ref, lq2_ref,
                      lk2_ref, g_ref, o_ref, qa_ref, base_ref, sa_ref, sb_ref, bma_ref, bmb_ref,
                      m_ref, acc_ref, diag_ref, *, lambda_init):
    tq = ATT_Q
    tk = ATT_K
    two_dk = 2 * DA_HEAD_DIM
    nk = k_ref.shape[0] // tk
    hd = pl.program_id(1)
    qi = pl.program_id(2)
    c = slopes_ref[hd] * LOG2E

    @pl.when(qi == 0)
    def _():
        kk = lax.broadcasted_iota(jnp.int32, (tk, tq), 0)
        qq = lax.broadcasted_iota(jnp.int32, (tk, tq), 1)
        diag_ref[...] = -c * jnp.abs(kk - qq).astype(F32)

    qt = qT_ref[...]
    zero = jnp.zeros((DA_HEAD_DIM, tq), BF16)
    qa_ref[:DA_HEAD_DIM, :] = jnp.concatenate([qt[:DA_HEAD_DIM], zero], axis=1)
    qa_ref[DA_HEAD_DIM:two_dk, :] = jnp.concatenate([zero, qt[DA_HEAD_DIM:]], axis=1)
    qa_ref[two_dk + _BIAS_ROWS:, :] = jnp.zeros((two_dk - _BIAS_ROWS, 2 * tq), BF16)

    lane = lax.broadcasted_iota(jnp.int32, (1, 2 * tq), 1)
    qpos = (qi * tq + jnp.where(lane >= tq, lane - tq, lane)).astype(F32)
    c1, c2, c3 = _bf16_split3(jnp.full((1, 2 * tq), c, F32))
    v1, v2, v3 = _bf16_split3(c * qpos)
    r = lax.broadcasted_iota(jnp.int32, (_BIAS_ROWS, 2 * tq), 0)
    base = jnp.zeros((_BIAS_ROWS, 2 * tq), F32)
    for idx, val in enumerate((-c1, -c2, -c3, -c1, -c2, -c3, v1, v2, v3)):
        base = jnp.where(r == idx, val, base)
    base_ref[...] = base

    ones_row = jnp.where(lax.broadcasted_iota(jnp.int32, (BF16_SUBLANE_TILE, tk), 0) == 0,
                         1.0, 0.0).astype(BF16)

    def block_start(t):
        return pl.multiple_of(lax.rem(qi + t, nk) * tk, tk)

    def scores(t, s_ref, bm_ref, diagonal=False):
        start = block_start(t)
        if diagonal:
            qa_ref[two_dk:two_dk + _BIAS_ROWS, :] = jnp.zeros((_BIAS_ROWS, 2 * tq), BF16)
        else:
            sign = jnp.where(start > qi * tq, 1.0, -1.0).astype(F32)
            qa_ref[two_dk:two_dk + _BIAS_ROWS, :] = (sign * base_ref[...]).astype(BF16)
        kb = jnp.concatenate([k_ref[pl.ds(start, tk), :], pos_ref[pl.ds(start, tk), :]], axis=1)
        s = _dot(kb, qa_ref[...])
        if diagonal:
            d = diag_ref[...]
            s = s + jnp.concatenate([d, d], axis=1)
        s_ref[...] = s
        bm_ref[...] = jnp.max(s, axis=0, keepdims=True)

    def consume(t, s_ref, bm_ref):
        m_old = m_ref[...]
        m_new = jnp.maximum(m_old, bm_ref[...])
        alpha = jnp.exp2(m_old - m_new)
        m_ref[...] = m_new
        p = jnp.exp2(s_ref[...] - m_new).astype(BF16)
        vb = jnp.concatenate([vT_ref[:, pl.ds(block_start(t), tk)], ones_row], axis=0)
        acc_ref[...] = alpha * acc_ref[...] + _dot(vb, p)

    acc_ref[...] = jnp.zeros_like(acc_ref)
    m_ref[...] = jnp.full(m_ref.shape, -1e30, F32)
    even, odd = (sa_ref, bma_ref), (sb_ref, bmb_ref)
    scores(0, *even, diagonal=True)

    def pair(i, carry):
        t = 2 * i
        scores(t + 1, *odd)
        consume(t, *even)
        scores(t + 2, *even)
        consume(t + 1, *odd)
        return carry

    lax.fori_loop(0, (nk - 2) // 2, pair, 0)
    scores(nk - 1, *odd)
    consume(nk - 2, *even)
    consume(nk - 1, *odd)

    lam = (jnp.exp(jnp.sum(lq1_ref[...] * lk1_ref[...], axis=-1, keepdims=True))
           - jnp.exp(jnp.sum(lq2_ref[...] * lk2_ref[...], axis=-1, keepdims=True))
           + lambda_init)
    inv_l = 1.0 / acc_ref[DA_V_DIM:DA_V_DIM + 1, :]
    oT = (acc_ref[:DA_V_DIM, :tq] * inv_l[:, :tq]
          - lam * (acc_ref[:DA_V_DIM, tq:] * inv_l[:, tq:]))
    o = oT.T
    ms = jnp.mean(o * o, axis=-1, keepdims=True)
    o_ref[...] = (o * lax.rsqrt(ms + LN_EPS) * g_ref[...] * (1.0 - lambda_init)).astype(o_ref.dtype)


def _key_positions(T):
    t = jnp.arange(T, dtype=jnp.int32)
    hi = ((t // _POS_SPLIT) * _POS_SPLIT).astype(F32)
    lo = (t % _POS_SPLIT).astype(F32)
    one = jnp.ones((T,), F32)
    cols = jnp.stack([hi, hi, hi, lo, lo, lo, one, one, one], axis=1)
    return jnp.pad(cols, ((0, 0), (0, 2 * DA_HEAD_DIM - cols.shape[1]))).astype(BF16)


def _diff_attn(slopes, qT, k, vT, lq1, lk1, lq2, lk2, g, lambda_init):
    B, _, T = qT.shape
    two_dk = 2 * DA_HEAD_DIM
    assert ATT_Q == ATT_K and T // ATT_K >= 2
    vec = lambda: pl.BlockSpec((1, DA_HEAD_DIM), lambda b, h, i: (0, 0))
    return pl.pallas_call(
        functools.partial(_diff_attn_kernel, lambda_init=lambda_init),
        grid=(B, DA_HEADS, T // ATT_Q),
        in_specs=[
            pl.BlockSpec(memory_space=pltpu.SMEM),
            pl.BlockSpec((None, two_dk, ATT_Q), lambda b, h, i: (b, h, i)),
            pl.BlockSpec((None, T, two_dk), lambda b, h, i: (b, 0, h)),
            pl.BlockSpec((T, two_dk), lambda b, h, i: (0, 0)),
            pl.BlockSpec((None, DA_V_DIM, T), lambda b, h, i: (b, h, 0)),
            vec(), vec(), vec(), vec(),
            pl.BlockSpec((1, DA_V_DIM), lambda b, h, i: (0, 0)),
        ],
        out_specs=pl.BlockSpec((None, ATT_Q, DA_V_DIM), lambda b, h, i: (b, i, h)),
        out_shape=jax.ShapeDtypeStruct((B, T, DA_WIDTH), BF16),
        scratch_shapes=[
            pltpu.VMEM((2 * two_dk, 2 * ATT_Q), BF16),
            pltpu.VMEM((_BIAS_ROWS, 2 * ATT_Q), F32),
            pltpu.VMEM((ATT_K, 2 * ATT_Q), F32),
            pltpu.VMEM((ATT_K, 2 * ATT_Q), F32),
            pltpu.VMEM((1, 2 * ATT_Q), F32),
            pltpu.VMEM((1, 2 * ATT_Q), F32),
            pltpu.VMEM((1, 2 * ATT_Q), F32),
            pltpu.VMEM((_ACC_ROWS, 2 * ATT_Q), F32),
            pltpu.VMEM((ATT_K, ATT_Q), F32),
        ],
        compiler_params=_params(("arbitrary", "arbitrary", "arbitrary")),
        name="diff_attn",
    )(slopes, qT, k, _key_positions(T), vT, lq1, lk1, lq2, lk2, g)


def _out_proj_kernel(h0_ref, yg_ref, yd_ref, ym_ref, w_ref, g_ref, b_ref, h1_ref, h1b_ref):
    y = (_dot(yg_ref[...], w_ref[:G_WIDTH])
         + _dot(yd_ref[...], w_ref[G_WIDTH:G_WIDTH + DA_WIDTH])
         + _dot(ym_ref[...], w_ref[G_WIDTH + DA_WIDTH:]))
    h1 = _layer_norm(ALPHA * h0_ref[...] + y, g_ref[...], b_ref[...])
    h1_ref[...] = h1
    h1b_ref[...] = h1.astype(BF16)


def _out_proj(h0, yg, yd, ym, w_out, g, b):
    N, D = h0.shape
    R = PROJ_ROWS
    rows = lambda width: pl.BlockSpec((R, width), lambda i: (i, 0))
    const = lambda shape: pl.BlockSpec(shape, lambda i: (0, 0))
    return pl.pallas_call(
        _out_proj_kernel,
        grid=(N // R,),
        in_specs=[rows(D), rows(G_WIDTH), rows(DA_WIDTH), rows(M_WIDTH),
                  const((D, D)), const((1, D)), const((1, D))],
        out_specs=[rows(D), rows(D)],
        out_shape=[jax.ShapeDtypeStruct((N, D), F32), jax.ShapeDtypeStruct((N, D), BF16)],
        compiler_params=_params(("arbitrary",)),
        name="out_proj",
    )(h0, yg, yd, ym, w_out, g, b)


_FFN_PAIR = 2 * FFN_COLS


def _conv_ffn_kernel(h1_ref, hb_ref, prev_ref, next_ref, wup_ref, cw_ref, cb_ref, wdn_ref,
                     g_ref, b_ref, o_ref, hext_ref, a0_ref, a1_ref, f_ref, *, steps_per_seq):
    rows = hb_ref.shape[0]
    n_chunks = wdn_ref.shape[0] // FFN_COLS
    i = pl.program_id(0) % steps_per_seq
    hext_ref[:HALO] = jnp.where(i == 0, jnp.zeros_like(prev_ref[...]), prev_ref[...])
    hext_ref[HALO:HALO + rows] = hb_ref[...]
    hext_ref[HALO + rows:] = jnp.where(i == steps_per_seq - 1, jnp.zeros_like(next_ref[...]),
                                       next_ref[...])

    def up(c, a_ref):
        cols = pl.ds(pl.multiple_of(c * _FFN_PAIR, _FFN_PAIR), _FFN_PAIR)
        a_ref[...] = _dot(hext_ref[...], wup_ref[:, cols])

    def down(c, a_ref, first=False):
        cols = pl.ds(pl.multiple_of(c * _FFN_PAIR, _FFN_PAIR), _FFN_PAIR)
        cw = cw_ref[:, cols]
        conv = (cw[0:1] * a_ref[HALO - 1:HALO - 1 + rows, :]
                + cw[1:2] * a_ref[HALO:HALO + rows, :]
                + cw[2:3] * a_ref[HALO + 1:HALO + 1 + rows, :]
                + cb_ref[:, cols])
        act = (_gelu(conv[:, :FFN_COLS]) * conv[:, FFN_COLS:]).astype(BF16)
        w = wdn_ref[pl.ds(pl.multiple_of(c * FFN_COLS, FFN_COLS), FFN_COLS), :]
        if first:
            f_ref[...] = _dot(act, w)
        else:
            f_ref[...] += _dot(act, w)

    up(0, a0_ref)
    up(1, a1_ref)
    down(0, a0_ref, first=True)

    def pair(p, carry):
        c = 2 * p + 1
        up(c + 1, a0_ref)
        down(c, a1_ref)
        up(c + 2, a1_ref)
        down(c + 1, a0_ref)
        return carry

    lax.fori_loop(0, (n_chunks - 3) // 2, pair, 0)
    up(n_chunks - 1, a0_ref)
    down(n_chunks - 2, a1_ref)
    down(n_chunks - 1, a0_ref)
    o_ref[...] = _layer_norm(ALPHA * h1_ref[...] + f_ref[...], g_ref[...], b_ref[...])


def _regroup_ffn_columns(a):
    lead = a.shape[:-1]
    a = a.reshape(*lead, 2, D_FF // FFN_COLS, FFN_COLS)
    return jnp.swapaxes(a, -3, -2).reshape(*lead, 2 * D_FF)


def _conv_ffn(h1, h1b, w_up, conv_w, conv_b, w_down, g, b, seq_len):
    N, D = h1.shape
    R = FFN_ROWS
    halo_per_step = R // HALO
    n_halo = N // HALO
    n_chunks = D_FF // FFN_COLS
    assert n_chunks >= 3 and n_chunks % 2 == 1
    const = lambda shape: pl.BlockSpec(shape, lambda i: (0, 0))
    return pl.pallas_call(
        functools.partial(_conv_ffn_kernel, steps_per_seq=seq_len // R),
        grid=(N // R,),
        in_specs=[
            pl.BlockSpec((R, D), lambda i: (i, 0)),
            pl.BlockSpec((R, D), lambda i: (i, 0)),
            pl.BlockSpec((HALO, D), lambda i: (jnp.maximum(i * halo_per_step - 1, 0), 0)),
            pl.BlockSpec((HALO, D), lambda i: (jnp.minimum((i + 1) * halo_per_step, n_halo - 1), 0)),
            const((D, 2 * D_FF)), const((CONV_W, 2 * D_FF)), const((1, 2 * D_FF)),
            const((D_FF, D)), const((1, D)), const((1, D)),
        ],
        out_specs=pl.BlockSpec((R, D), lambda i: (i, 0)),
        out_shape=jax.ShapeDtypeStruct((N, D), F32),
        scratch_shapes=[
            pltpu.VMEM((R + 2 * HALO, D), BF16),
            pltpu.VMEM((R + 2 * HALO, _FFN_PAIR), F32),
            pltpu.VMEM((R + 2 * HALO, _FFN_PAIR), F32),
            pltpu.VMEM((R, D), F32),
        ],
        compiler_params=_params(("arbitrary",)),
        name="conv_ffn",
    )(h1, h1b, h1b, h1b, _regroup_ffn_columns(w_up), _regroup_ffn_columns(conv_w),
      _regroup_ffn_columns(conv_b), w_down, g, b)


def _alibi_slopes(n):
    return jnp.asarray([2.0 ** (-8.0 * (i + 1) / n) for i in range(n)], F32)


def kernel(x, mem, ln_emb_g, ln_emb_b, w_in, gmlp_ln_g, gmlp_ln_b, gmlp_ws, gmlp_bs, lambda_q1, lambda_k1, lambda_q2, lambda_k2, da_subln_g, mem_ln_g, mem_ln_b, w_mem_kv, w_out, ln1_g, ln1_b, w_up, conv_w, conv_b, w_down, ln2_g, ln2_b):
    B, T, D = x.shape
    assert DEPTH == 1 and w_in.shape[0] == 1
    assert T % PROJ_ROWS == 0 and T % ATT_Q == 0 and T % ATT_K == 0 and T % FFN_ROWS == 0
    assert D_FF % FFN_COLS == 0
    l = 0
    lambda_init = 0.8 - 0.6 * math.exp(-0.3 * l)
    row = lambda a: a.reshape(1, -1)

    w = w_in[l]
    u_end, v_end = G_WIDTH, 2 * G_WIDTH
    q_end = v_end + DA_QK
    k_end = q_end + DA_QK
    vd_end = k_end + DA_WIDTH
    w_a = jnp.concatenate([w[:, :v_end], w[:, q_end:k_end], w[:, vd_end:]], axis=1).astype(BF16)
    w_bT = jnp.concatenate([w[:, v_end:q_end], w[:, k_end:vd_end]], axis=1).T.astype(BF16)
    w_kmT = w_mem_kv[l][:, :M_WIDTH].T.astype(BF16)
    w_vm = w_mem_kv[l][:, M_WIDTH:].astype(BF16)
    ws = gmlp_ws[l].reshape(G_HEADS * CHUNK, CHUNK).astype(BF16)
    bs_tile = jnp.repeat(gmlp_bs[l].T, G_DIM, axis=1)

    kmT, vm = _mem_kv(mem, row(mem_ln_g[l]), row(mem_ln_b[l]), w_kmT, w_vm)
    h0, yg, ym, qT, k, vT = _in_proj(
        x, row(ln_emb_g), row(ln_emb_b), w_a, w_bT, row(gmlp_ln_g[l]), row(gmlp_ln_b[l]),
        ws, bs_tile, kmT, vm)
    yd = _diff_attn(_alibi_slopes(DA_HEADS), qT, k, vT, row(lambda_q1[l]), row(lambda_k1[l]),
                    row(lambda_q2[l]), row(lambda_k2[l]), row(da_subln_g[l]), lambda_init)

    N = B * T
    h1, h1b = _out_proj(h0.reshape(N, D), yg.reshape(N, G_WIDTH), yd.reshape(N, DA_WIDTH),
                        ym.reshape(N, M_WIDTH), w_out[l].astype(BF16), row(ln1_g[l]), row(ln1_b[l]))
    out = _conv_ffn(h1, h1b, w_up[l].astype(BF16), conv_w[l], row(conv_b[l]),
                    w_down[l].astype(BF16), row(ln2_g[l]), row(ln2_b[l]), T)
    return out.reshape(B, T, D)
```

```python
import functools
import math

import jax
import jax.numpy as jnp
from jax import lax
from jax.experimental import pallas as pl
from jax.experimental.pallas import tpu as pltpu

F32 = jnp.float32
BF16 = jnp.bfloat16

D_MODEL = 1024
DEPTH = 1
MEM_LEN = 256
CHUNK = 128
G_HEADS = 4
G_WIDTH = D_MODEL // 4
G_DIM = G_WIDTH // G_HEADS
DA_HEADS = 4
DA_WIDTH = D_MODEL // 2
DA_V_DIM = DA_WIDTH // DA_HEADS
DA_HEAD_DIM = DA_V_DIM // 2
DA_QK = DA_HEADS * 2 * DA_HEAD_DIM
M_HEADS = 4
M_WIDTH = D_MODEL // 4
M_HEAD_DIM = M_WIDTH // M_HEADS
D_FF = 2816
CONV_W = 3
LN_EPS = 1e-5
ALPHA = (2.0 * DEPTH) ** 0.25
LOG2E = math.log2(math.e)

V7X_VMEM_LIMIT_BYTES = 56 * 1024 * 1024
BF16_SUBLANE_TILE = 16

PROJ_ROWS = 512
ATT_Q = 1024
ATT_K = 512
FFN_ROWS = 512
FFN_COLS = 256
HALO = BF16_SUBLANE_TILE

NT_DIMS = (((1,), (1,)), ((), ()))


def _layer_norm(x, g, b):
    mu = jnp.mean(x, axis=-1, keepdims=True)
    xc = x - mu
    var = jnp.mean(xc * xc, axis=-1, keepdims=True)
    return xc * lax.rsqrt(var + LN_EPS) * g + b


def _gelu(x):
    return 0.5 * x * (1.0 + lax.erf(x * (2.0 ** -0.5)))


def _dot(a, b):
    return jnp.dot(a, b, preferred_element_type=F32)


def _params(semantics):
    return pltpu.CompilerParams(dimension_semantics=semantics,
                                vmem_limit_bytes=V7X_VMEM_LIMIT_BYTES)


def _mem_kv_kernel(mem_ref, g_ref, b_ref, wkT_ref, wv_ref, kmT_ref, vm_ref):
    m = _layer_norm(mem_ref[...], g_ref[...], b_ref[...]).astype(BF16)
    kT = lax.dot_general(wkT_ref[...], m, NT_DIMS, preferred_element_type=F32)
    kT = kT * (M_HEAD_DIM ** -0.5)
    v = _dot(m, wv_ref[...])
    head_of_row = lax.broadcasted_iota(jnp.int32, (M_WIDTH, MEM_LEN), 0) // M_HEAD_DIM
    head_of_col = lax.broadcasted_iota(jnp.int32, (MEM_LEN, M_WIDTH), 1) // M_HEAD_DIM
    for h in range(M_HEADS):
        kmT_ref[h] = jnp.where(head_of_row == h, kT, 0.0).astype(BF16)
        vm_ref[h] = jnp.where(head_of_col == h, v, 0.0).astype(BF16)


def _mem_kv(mem, g, b, wkT, wv):
    B = mem.shape[0]
    row = lambda: pl.BlockSpec((1, D_MODEL), lambda i: (0, 0))
    return pl.pallas_call(
        _mem_kv_kernel,
        grid=(B,),
        in_specs=[
            pl.BlockSpec((None, MEM_LEN, D_MODEL), lambda i: (i, 0, 0)),
            row(), row(),
            pl.BlockSpec((M_WIDTH, D_MODEL), lambda i: (0, 0)),
            pl.BlockSpec((D_MODEL, M_WIDTH), lambda i: (0, 0)),
        ],
        out_specs=[
            pl.BlockSpec((None, M_HEADS, M_WIDTH, MEM_LEN), lambda i: (i, 0, 0, 0)),
            pl.BlockSpec((None, M_HEADS, MEM_LEN, M_WIDTH), lambda i: (i, 0, 0, 0)),
        ],
        out_shape=[
            jax.ShapeDtypeStruct((B, M_HEADS, M_WIDTH, MEM_LEN), BF16),
            jax.ShapeDtypeStruct((B, M_HEADS, MEM_LEN, M_WIDTH), BF16),
        ],
        compiler_params=_params(("arbitrary",)),
        name="mem_kv",
    )(mem, g, b, wkT, wv)


_A_U = 0
_A_V = G_WIDTH
_A_K = 2 * G_WIDTH
_A_QM = 2 * G_WIDTH + DA_QK
_A_END = _A_QM + M_WIDTH


def _in_proj_kernel(x_ref, eg_ref, eb_ref, wa_ref, wbT_ref, gg_ref, gb_ref, ws_ref, bs_ref,
                    kmT_ref, vm_ref,
                    h0_ref, yg_ref, ym_ref, qT_ref, k_ref, vT_ref):
    rows = x_ref.shape[0]
    h = _layer_norm(x_ref[...], eg_ref[...], eb_ref[...])
    h0_ref[...] = h
    hb = h.astype(BF16)

    pa = _dot(hb, wa_ref[...])
    pbT = lax.dot_general(wbT_ref[...], hb, NT_DIMS, preferred_element_type=F32)
    qT_ref[...] = (pbT[:DA_QK] * (DA_HEAD_DIM ** -0.5 * LOG2E)).astype(BF16)
    vT_ref[...] = pbT[DA_QK:].astype(BF16)
    k_ref[...] = pa[:, _A_K:_A_QM].astype(BF16)

    u = _gelu(pa[:, _A_U:_A_V])
    v = _layer_norm(_gelu(pa[:, _A_V:_A_K]), gg_ref[...], gb_ref[...]).astype(BF16)
    group_of_lane = lax.broadcasted_iota(jnp.int32, (CHUNK, G_WIDTH), 1) // G_DIM
    ws = ws_ref[...]
    bs = bs_ref[...]
    for c in range(rows // CHUNK):
        sl = slice(c * CHUNK, (c + 1) * CHUNK)
        r = _dot(ws, v[sl])
        s = r[(G_HEADS - 1) * CHUNK:]
        for g in range(G_HEADS - 2, -1, -1):
            s = jnp.where(group_of_lane == g, r[g * CHUNK:(g + 1) * CHUNK], s)
        yg_ref[sl, :] = (u[sl] * (s + bs)).astype(BF16)

    qm = pa[:, _A_QM:_A_END].astype(BF16)
    o = jnp.zeros((rows, M_WIDTH), F32)
    for hd in range(M_HEADS):
        s = _dot(qm, kmT_ref[hd])
        p = jnp.exp(s - jnp.max(s, axis=-1, keepdims=True))
        p = p / jnp.sum(p, axis=-1, keepdims=True)
        o = o + _dot(p.astype(BF16), vm_ref[hd])
    ym_ref[...] = o.astype(BF16)


def _in_proj(x, eg, eb, w_a, w_bT, gg, gb, ws, bs_tile, kmT, vm):
    B, T, D = x.shape
    R = PROJ_ROWS
    const2 = lambda shape: pl.BlockSpec(shape, lambda b, i: (0, 0))
    return pl.pallas_call(
        _in_proj_kernel,
        grid=(B, T // R),
        in_specs=[
            pl.BlockSpec((None, R, D), lambda b, i: (b, i, 0)),
            const2((1, D)), const2((1, D)),
            const2((D, _A_END)),
            const2((DA_QK + DA_WIDTH, D)),
            const2((1, G_WIDTH)), const2((1, G_WIDTH)),
            const2((G_HEADS * CHUNK, CHUNK)),
            const2((CHUNK, G_WIDTH)),
            pl.BlockSpec((None, M_HEADS, M_WIDTH, MEM_LEN), lambda b, i: (b, 0, 0, 0)),
            pl.BlockSpec((None, M_HEADS, MEM_LEN, M_WIDTH), lambda b, i: (b, 0, 0, 0)),
        ],
        out_specs=[
            pl.BlockSpec((None, R, D), lambda b, i: (b, i, 0)),
            pl.BlockSpec((None, R, G_WIDTH), lambda b, i: (b, i, 0)),
            pl.BlockSpec((None, R, M_WIDTH), lambda b, i: (b, i, 0)),
            pl.BlockSpec((None, DA_QK, R), lambda b, i: (b, 0, i)),
            pl.BlockSpec((None, R, DA_QK), lambda b, i: (b, i, 0)),
            pl.BlockSpec((None, DA_WIDTH, R), lambda b, i: (b, 0, i)),
        ],
        out_shape=[
            jax.ShapeDtypeStruct((B, T, D), F32),
            jax.ShapeDtypeStruct((B, T, G_WIDTH), BF16),
            jax.ShapeDtypeStruct((B, T, M_WIDTH), BF16),
            jax.ShapeDtypeStruct((B, DA_QK, T), BF16),
            jax.ShapeDtypeStruct((B, T, DA_QK), BF16),
            jax.ShapeDtypeStruct((B, DA_WIDTH, T), BF16),
        ],
        compiler_params=_params(("arbitrary", "arbitrary")),
        name="in_proj",
    )(x, eg, eb, w_a, w_bT, gg, gb, ws, bs_tile, kmT, vm)


_POS_SPLIT = 64
_BIAS_ROWS = BF16_SUBLANE_TILE
_ACC_ROWS = DA_V_DIM + BF16_SUBLANE_TILE


def _bf16_split3(x):
    a = x.astype(BF16).astype(F32)
    b = (x - a).astype(BF16).astype(F32)
    c = x - a - b
    return a, b, c


def _diff_attn_kernel(slopes_ref, qT_ref, k_ref, pos_ref, vT_ref, lq1_ref, lk1_ref, lq2_ref,
                      lk2_ref, g_ref, o_ref, qa_ref, base_ref, sa_ref, sb_ref, bma_ref, bmb_ref,
                      m_ref, acc_ref, diag_ref, *, lambda_init):
    tq = ATT_Q
    tk = ATT_K
    two_dk = 2 * DA_HEAD_DIM
    nk = k_ref.shape[0] // tk
    nq = qT_ref.shape[1] // tq
    sub = tq // tk
    c = slopes_ref[pl.program_id(1)] * LOG2E

    kk = lax.broadcasted_iota(jnp.int32, (tk, tk), 0)
    qq = lax.broadcasted_iota(jnp.int32, (tk, tk), 1)
    diag_ref[...] = -c * jnp.abs(kk - qq).astype(F32)

    lane = lax.broadcasted_iota(jnp.int32, (1, 2 * tq), 1)
    col = jnp.where(lane >= tq, lane - tq, lane)
    group = col // tk

    lam = (jnp.exp(jnp.sum(lq1_ref[...] * lk1_ref[...], axis=-1, keepdims=True))
           - jnp.exp(jnp.sum(lq2_ref[...] * lk2_ref[...], axis=-1, keepdims=True))
           + lambda_init)
    ones_row = jnp.where(lax.broadcasted_iota(jnp.int32, (BF16_SUBLANE_TILE, tk), 0) == 0,
                         1.0, 0.0).astype(BF16)
    qa_ref[two_dk + _BIAS_ROWS:, :] = jnp.zeros((two_dk - _BIAS_ROWS, 2 * tq), BF16)

    def load_queries(qn):
        qt = qT_ref[:, pl.ds(pl.multiple_of(qn * tq, tq), tq)]
        zero = jnp.zeros((DA_HEAD_DIM, tq), BF16)
        qa_ref[:DA_HEAD_DIM, :] = jnp.concatenate([qt[:DA_HEAD_DIM], zero], axis=1)
        qa_ref[DA_HEAD_DIM:two_dk, :] = jnp.concatenate([zero, qt[DA_HEAD_DIM:]], axis=1)
        qpos = (qn * tq + col).astype(F32)
        c1, c2, c3 = _bf16_split3(jnp.full((1, 2 * tq), c, F32))
        v1, v2, v3 = _bf16_split3(c * qpos)
        r = lax.broadcasted_iota(jnp.int32, (_BIAS_ROWS, 2 * tq), 0)
        base = jnp.zeros((_BIAS_ROWS, 2 * tq), F32)
        for idx, val in enumerate((-c1, -c2, -c3, -c1, -c2, -c3, v1, v2, v3)):
            base = jnp.where(r == idx, val, base)
        base_ref[...] = base

    def block_start(qn, t):
        return pl.multiple_of(lax.rem(qn * sub + t, nk) * tk, tk)

    n_strips = 2 * sub

    def step(score_args, consume_args):
        if score_args is not None:
            qn, t, s_ref, bm_ref, straddles = score_args
            start = block_start(qn, t)
            if straddles:
                sign = jnp.where(group < t, 1.0, jnp.where(group > t, -1.0, 0.0)).astype(F32)
            else:
                sign = jnp.where(start > qn * tq, 1.0, -1.0).astype(F32)
            qa_ref[two_dk:two_dk + _BIAS_ROWS, :] = (sign * base_ref[...]).astype(BF16)
            kb = jnp.concatenate([k_ref[pl.ds(start, tk), :], pos_ref[pl.ds(start, tk), :]],
                                 axis=1)
        if consume_args is not None:
            cqn, ct, cs_ref, cbm_ref = consume_args
            vb = jnp.concatenate([vT_ref[:, pl.ds(block_start(cqn, ct), tk)], ones_row], axis=0)
        for j in range(n_strips):
            cols = slice(j * tk, (j + 1) * tk)
            if score_args is not None:
                s = _dot(kb, qa_ref[:, cols])
                if straddles and j % sub == t:
                    s = s + diag_ref[...]
                s_ref[:, cols] = s
                bm_ref[:, cols] = jnp.max(s, axis=0, keepdims=True)
            if consume_args is not None:
                m_old = m_ref[:, cols]
                m_new = jnp.maximum(m_old, cbm_ref[:, cols])
                alpha = jnp.exp2(m_old - m_new)
                m_ref[:, cols] = m_new
                p = jnp.exp2(cs_ref[:, cols] - m_new).astype(BF16)
                acc_ref[:, cols] = alpha * acc_ref[:, cols] + _dot(vb, p)

    def reset_stats():
        acc_ref[...] = jnp.zeros_like(acc_ref)
        m_ref[...] = jnp.full(m_ref.shape, -1e30, F32)

    def finish(qn):
        inv_l = 1.0 / acc_ref[DA_V_DIM:DA_V_DIM + 1, :]
        oT = (acc_ref[:DA_V_DIM, :tq] * inv_l[:, :tq]
              - lam * (acc_ref[:DA_V_DIM, tq:] * inv_l[:, tq:]))
        o = oT.T
        ms = jnp.mean(o * o, axis=-1, keepdims=True)
        o = o * lax.rsqrt(ms + LN_EPS) * g_ref[...] * (1.0 - lambda_init)
        o_ref[pl.ds(pl.multiple_of(qn * tq, tq), tq), :] = o.astype(o_ref.dtype)

    bufs = ((sa_ref, bma_ref), (sb_ref, bmb_ref))

    last = (nk - 1) % 2

    def head_rest(qn):
        for t in range(1, sub):
            step((qn, t) + bufs[t % 2] + (True,), (qn, t - 1) + bufs[(t - 1) % 2])

    def body(qn):
        def pair(i, inner):
            t = sub + 2 * i
            step((qn, t) + bufs[sub % 2] + (False,), (qn, t - 1) + bufs[(sub - 1) % 2])
            step((qn, t + 1) + bufs[(sub + 1) % 2] + (False,), (qn, t) + bufs[sub % 2])
            return inner

        lax.fori_loop(0, (nk - sub) // 2, pair, 0)

    load_queries(0)
    step((0, 0) + bufs[0] + (True,), None)
    reset_stats()
    head_rest(0)

    def query_block(qn, carry):
        body(qn)
        load_queries(qn + 1)
        step((qn + 1, 0) + bufs[0] + (True,), (qn, nk - 1) + bufs[last])
        finish(qn)
        reset_stats()
        head_rest(qn + 1)
        return carry

    lax.fori_loop(0, nq - 1, query_block, 0)
    body(nq - 1)
    step(None, (nq - 1, nk - 1) + bufs[last])
    finish(nq - 1)


def _key_positions(T):
    t = jnp.arange(T, dtype=jnp.int32)
    hi = ((t // _POS_SPLIT) * _POS_SPLIT).astype(F32)
    lo = (t % _POS_SPLIT).astype(F32)
    one = jnp.ones((T,), F32)
    cols = jnp.stack([hi, hi, hi, lo, lo, lo, one, one, one], axis=1)
    return jnp.pad(cols, ((0, 0), (0, 2 * DA_HEAD_DIM - cols.shape[1]))).astype(BF16)


def _diff_attn(slopes, qT, k, vT, lq1, lk1, lq2, lk2, g, lambda_init):
    B, _, T = qT.shape
    two_dk = 2 * DA_HEAD_DIM
    assert ATT_Q % ATT_K == 0 and T % ATT_Q == 0 and T // ATT_Q >= 2
    assert (T // ATT_K - ATT_Q // ATT_K) % 2 == 0
    vec = lambda: pl.BlockSpec((1, DA_HEAD_DIM), lambda b, h: (0, 0))
    return pl.pallas_call(
        functools.partial(_diff_attn_kernel, lambda_init=lambda_init),
        grid=(B, DA_HEADS),
        in_specs=[
            pl.BlockSpec(memory_space=pltpu.SMEM),
            pl.BlockSpec((None, two_dk, T), lambda b, h: (b, h, 0)),
            pl.BlockSpec((None, T, two_dk), lambda b, h: (b, 0, h)),
            pl.BlockSpec((T, two_dk), lambda b, h: (0, 0)),
            pl.BlockSpec((None, DA_V_DIM, T), lambda b, h: (b, h, 0)),
            vec(), vec(), vec(), vec(),
            pl.BlockSpec((1, DA_V_DIM), lambda b, h: (0, 0)),
        ],
        out_specs=pl.BlockSpec((None, T, DA_V_DIM), lambda b, h: (b, 0, h)),
        out_shape=jax.ShapeDtypeStruct((B, T, DA_WIDTH), BF16),
        scratch_shapes=[
            pltpu.VMEM((2 * two_dk, 2 * ATT_Q), BF16),
            pltpu.VMEM((_BIAS_ROWS, 2 * ATT_Q), F32),
            pltpu.VMEM((ATT_K, 2 * ATT_Q), F32),
            pltpu.VMEM((ATT_K, 2 * ATT_Q), F32),
            pltpu.VMEM((1, 2 * ATT_Q), F32),
            pltpu.VMEM((1, 2 * ATT_Q), F32),
            pltpu.VMEM((1, 2 * ATT_Q), F32),
            pltpu.VMEM((_ACC_ROWS, 2 * ATT_Q), F32),
            pltpu.VMEM((ATT_K, ATT_K), F32),
        ],
        compiler_params=_params(("arbitrary", "arbitrary")),
        name="diff_attn",
    )(slopes, qT, k, _key_positions(T), vT, lq1, lk1, lq2, lk2, g)


def _out_proj_kernel(h0_ref, yg_ref, yd_ref, ym_ref, w_ref, g_ref, b_ref, h1_ref):
    y = (_dot(yg_ref[...], w_ref[:G_WIDTH])
         + _dot(yd_ref[...], w_ref[G_WIDTH:G_WIDTH + DA_WIDTH])
         + _dot(ym_ref[...], w_ref[G_WIDTH + DA_WIDTH:]))
    h1_ref[...] = _layer_norm(ALPHA * h0_ref[...] + y, g_ref[...], b_ref[...])


def _out_proj(h0, yg, yd, ym, w_out, g, b):
    N, D = h0.shape
    R = PROJ_ROWS
    rows = lambda width: pl.BlockSpec((R, width), lambda i: (i, 0))
    const = lambda shape: pl.BlockSpec(shape, lambda i: (0, 0))
    return pl.pallas_call(
        _out_proj_kernel,
        grid=(N // R,),
        in_specs=[rows(D), rows(G_WIDTH), rows(DA_WIDTH), rows(M_WIDTH),
                  const((D, D)), const((1, D)), const((1, D))],
        out_specs=rows(D),
        out_shape=jax.ShapeDtypeStruct((N, D), F32),
        compiler_params=_params(("arbitrary",)),
        name="out_proj",
    )(h0, yg, yd, ym, w_out, g, b)


_FFN_SPLIT = 2


def _conv_ffn_kernel(h1_ref, prev_ref, next_ref, wup_ref, cw_ref, cb_ref, wdn_ref,
                     g_ref, b_ref, o_ref, hext_ref, a0_ref, a1_ref, f_ref, *, steps_per_seq):
    rows = h1_ref.shape[0]
    n_chunks = wdn_ref.shape[0] // FFN_COLS
    i = pl.program_id(0) % steps_per_seq
    prev = prev_ref[...].astype(BF16)
    nxt = next_ref[...].astype(BF16)
    hext_ref[:HALO] = jnp.where(i == 0, jnp.zeros_like(prev), prev)
    hext_ref[HALO:HALO + rows] = h1_ref[...].astype(BF16)
    hext_ref[HALO + rows:] = jnp.where(i == steps_per_seq - 1, jnp.zeros_like(nxt), nxt)

    def columns(c):
        gate = pl.ds(pl.multiple_of(c * FFN_COLS, FFN_COLS), FFN_COLS)
        val = pl.ds(pl.multiple_of(D_FF + c * FFN_COLS, FFN_COLS), FFN_COLS)
        return gate, val

    def stage(up, down, first=False):
        ext_rows = rows + 2 * HALO
        up_rows = ext_rows // _FFN_SPLIT
        down_rows = rows // (2 * _FFN_SPLIT)
        if up is not None:
            up_cols = columns(up[0])

            def project(piece):
                half, part = divmod(piece, _FFN_SPLIT)
                r0 = part * up_rows
                up[1][half, r0:r0 + up_rows, :] = _dot(hext_ref[r0:r0 + up_rows, :],
                                                       wup_ref[:, up_cols[half]])

        if down is not None:
            c, a_ref = down
            down_cols = columns(c)
            w = wdn_ref[pl.ds(pl.multiple_of(c * FFN_COLS, FFN_COLS), FFN_COLS), :]

            def conv(half, r0):
                cols = down_cols[half]
                cw = cw_ref[:, cols]
                lo = HALO + r0
                return (cw[0:1] * a_ref[half, lo - 1:lo - 1 + down_rows, :]
                        + cw[1:2] * a_ref[half, lo:lo + down_rows, :]
                        + cw[2:3] * a_ref[half, lo + 1:lo + 1 + down_rows, :]
                        + cb_ref[:, cols])

            def contract(piece):
                r0 = piece * down_rows
                act = (_gelu(conv(0, r0)) * conv(1, r0)).astype(BF16)
                if first:
                    f_ref[r0:r0 + down_rows, :] = _dot(act, w)
                else:
                    f_ref[r0:r0 + down_rows, :] += _dot(act, w)

        for piece in range(2 * _FFN_SPLIT):
            if up is not None:
                project(piece)
            if down is not None:
                contract(piece)

    stage((0, a0_ref), None)
    stage((1, a1_ref), (0, a0_ref), first=True)

    def pair(p, carry):
        c = 2 * p + 1
        stage((c + 1, a0_ref), (c, a1_ref))
        stage((c + 2, a1_ref), (c + 1, a0_ref))
        return carry

    lax.fori_loop(0, (n_chunks - 3) // 2, pair, 0)
    stage((n_chunks - 1, a0_ref), (n_chunks - 2, a1_ref))
    stage(None, (n_chunks - 1, a0_ref))
    o_ref[...] = _layer_norm(ALPHA * h1_ref[...] + f_ref[...], g_ref[...], b_ref[...])


def _conv_ffn(h1, w_up, conv_w, conv_b, w_down, g, b, seq_len):
    N, D = h1.shape
    R = FFN_ROWS
    halo_per_step = R // HALO
    n_halo = N // HALO
    n_chunks = D_FF // FFN_COLS
    assert n_chunks >= 3 and n_chunks % 2 == 1
    const = lambda shape: pl.BlockSpec(shape, lambda i: (0, 0))
    return pl.pallas_call(
        functools.partial(_conv_ffn_kernel, steps_per_seq=seq_len // R),
        grid=(N // R,),
        in_specs=[
            pl.BlockSpec((R, D), lambda i: (i, 0)),
            pl.BlockSpec((HALO, D), lambda i: (jnp.maximum(i * halo_per_step - 1, 0), 0)),
            pl.BlockSpec((HALO, D), lambda i: (jnp.minimum((i + 1) * halo_per_step, n_halo - 1), 0)),
            const((D, 2 * D_FF)), const((CONV_W, 2 * D_FF)), const((1, 2 * D_FF)),
            const((D_FF, D)), const((1, D)), const((1, D)),
        ],
        out_specs=pl.BlockSpec((R, D), lambda i: (i, 0)),
        out_shape=jax.ShapeDtypeStruct((N, D), F32),
        scratch_shapes=[
            pltpu.VMEM((R + 2 * HALO, D), BF16),
            pltpu.VMEM((2, R + 2 * HALO, FFN_COLS), F32),
            pltpu.VMEM((2, R + 2 * HALO, FFN_COLS), F32),
            pltpu.VMEM((R, D), F32),
        ],
        compiler_params=_params(("arbitrary",)),
        name="conv_ffn",
    )(h1, h1, h1, w_up, conv_w, conv_b, w_down, g, b)


def _alibi_slopes(n):
    return jnp.asarray([2.0 ** (-8.0 * (i + 1) / n) for i in range(n)], F32)


def kernel(x, mem, ln_emb_g, ln_emb_b, w_in, gmlp_ln_g, gmlp_ln_b, gmlp_ws, gmlp_bs, lambda_q1, lambda_k1, lambda_q2, lambda_k2, da_subln_g, mem_ln_g, mem_ln_b, w_mem_kv, w_out, ln1_g, ln1_b, w_up, conv_w, conv_b, w_down, ln2_g, ln2_b):
    B, T, D = x.shape
    assert DEPTH == 1 and w_in.shape[0] == 1
    assert T % PROJ_ROWS == 0 and T % ATT_Q == 0 and T % ATT_K == 0 and T % FFN_ROWS == 0
    assert D_FF % FFN_COLS == 0
    l = 0
    lambda_init = 0.8 - 0.6 * math.exp(-0.3 * l)
    row = lambda a: a.reshape(1, -1)

    w = w_in[l]
    u_end, v_end = G_WIDTH, 2 * G_WIDTH
    q_end = v_end + DA_QK
    k_end = q_end + DA_QK
    vd_end = k_end + DA_WIDTH
    w_a = jnp.concatenate([w[:, :v_end], w[:, q_end:k_end], w[:, vd_end:]], axis=1).astype(BF16)
    w_bT = jnp.concatenate([w[:, v_end:q_end], w[:, k_end:vd_end]], axis=1).T.astype(BF16)
    w_kmT = w_mem_kv[l][:, :M_WIDTH].T.astype(BF16)
    w_vm = w_mem_kv[l][:, M_WIDTH:].astype(BF16)
    ws = gmlp_ws[l].reshape(G_HEADS * CHUNK, CHUNK).astype(BF16)
    bs_tile = jnp.repeat(gmlp_bs[l].T, G_DIM, axis=1)

    kmT, vm = _mem_kv(mem, row(mem_ln_g[l]), row(mem_ln_b[l]), w_kmT, w_vm)
    h0, yg, ym, qT, k, vT = _in_proj(
        x, row(ln_emb_g), row(ln_emb_b), w_a, w_bT, row(gmlp_ln_g[l]), row(gmlp_ln_b[l]),
        ws, bs_tile, kmT, vm)
    yd = _diff_attn(_alibi_slopes(DA_HEADS), qT, k, vT, row(lambda_q1[l]), row(lambda_k1[l]),
                    row(lambda_q2[l]), row(lambda_k2[l]), row(da_subln_g[l]), lambda_init)

    N = B * T
    h1 = _out_proj(h0.reshape(N, D), yg.reshape(N, G_WIDTH), yd.reshape(N, DA_WIDTH),
                   ym.reshape(N, M_WIDTH), w_out[l].astype(BF16), row(ln1_g[l]), row(ln1_b[l]))
    out = _conv_ffn(h1, w_up[l].astype(BF16), conv_w[l], row(conv_b[l]),
                    w_down[l].astype(BF16), row(ln2_g[l]), row(ln2_b[l]), T)
    return out.reshape(B, T, D)
```

```python
import functools
import math

import jax
import jax.numpy as jnp
from jax import lax
from jax.experimental import pallas as pl
from jax.experimental.pallas import tpu as pltpu

F32 = jnp.float32
BF16 = jnp.bfloat16

D_MODEL = 1024
DEPTH = 1
MEM_LEN = 256
CHUNK = 128
G_HEADS = 4
G_WIDTH = D_MODEL // 4
G_DIM = G_WIDTH // G_HEADS
DA_HEADS = 4
DA_WIDTH = D_MODEL // 2
DA_V_DIM = DA_WIDTH // DA_HEADS
DA_HEAD_DIM = DA_V_DIM // 2
DA_QK = DA_HEADS * 2 * DA_HEAD_DIM
M_HEADS = 4
M_WIDTH = D_MODEL // 4
M_HEAD_DIM = M_WIDTH // M_HEADS
D_FF = 2816
CONV_W = 3
LN_EPS = 1e-5
ALPHA = (2.0 * DEPTH) ** 0.25
LOG2E = math.log2(math.e)

V7X_VMEM_LIMIT_BYTES = 56 * 1024 * 1024
BF16_SUBLANE_TILE = 16

PROJ_ROWS = 512
ATT_Q = 1024
ATT_K = 512
FFN_ROWS = 512
FFN_COLS = 256
HALO = BF16_SUBLANE_TILE

NT_DIMS = (((1,), (1,)), ((), ()))


def _layer_norm(x, g, b):
    mu = jnp.mean(x, axis=-1, keepdims=True)
    xc = x - mu
    var = jnp.mean(xc * xc, axis=-1, keepdims=True)
    return xc * lax.rsqrt(var + LN_EPS) * g + b


def _gelu(x):
    return 0.5 * x * (1.0 + lax.erf(x * (2.0 ** -0.5)))


def _dot(a, b):
    return jnp.dot(a, b, preferred_element_type=F32)


def _params(semantics):
    return pltpu.CompilerParams(dimension_semantics=semantics,
                                vmem_limit_bytes=V7X_VMEM_LIMIT_BYTES)


def _mem_kv_kernel(mem_ref, g_ref, b_ref, wkT_ref, wv_ref, kmT_ref, vm_ref):
    m = _layer_norm(mem_ref[...], g_ref[...], b_ref[...]).astype(BF16)
    kT = lax.dot_general(wkT_ref[...], m, NT_DIMS, preferred_element_type=F32)
    kT = kT * (M_HEAD_DIM ** -0.5)
    v = _dot(m, wv_ref[...])
    head_of_row = lax.broadcasted_iota(jnp.int32, (M_WIDTH, MEM_LEN), 0) // M_HEAD_DIM
    head_of_col = lax.broadcasted_iota(jnp.int32, (MEM_LEN, M_WIDTH), 1) // M_HEAD_DIM
    for h in range(M_HEADS):
        kmT_ref[h] = jnp.where(head_of_row == h, kT, 0.0).astype(BF16)
        vm_ref[h] = jnp.where(head_of_col == h, v, 0.0).astype(BF16)


def _mem_kv(mem, g, b, wkT, wv):
    B = mem.shape[0]
    row = lambda: pl.BlockSpec((1, D_MODEL), lambda i: (0, 0))
    return pl.pallas_call(
        _mem_kv_kernel,
        grid=(B,),
        in_specs=[
            pl.BlockSpec((None, MEM_LEN, D_MODEL), lambda i: (i, 0, 0)),
            row(), row(),
            pl.BlockSpec((M_WIDTH, D_MODEL), lambda i: (0, 0)),
            pl.BlockSpec((D_MODEL, M_WIDTH), lambda i: (0, 0)),
        ],
        out_specs=[
            pl.BlockSpec((None, M_HEADS, M_WIDTH, MEM_LEN), lambda i: (i, 0, 0, 0)),
            pl.BlockSpec((None, M_HEADS, MEM_LEN, M_WIDTH), lambda i: (i, 0, 0, 0)),
        ],
        out_shape=[
            jax.ShapeDtypeStruct((B, M_HEADS, M_WIDTH, MEM_LEN), BF16),
            jax.ShapeDtypeStruct((B, M_HEADS, MEM_LEN, M_WIDTH), BF16),
        ],
        compiler_params=_params(("arbitrary",)),
        name="mem_kv",
    )(mem, g, b, wkT, wv)


_A_U = 0
_A_V = G_WIDTH
_A_K = 2 * G_WIDTH
_A_QM = 2 * G_WIDTH + DA_QK
_A_END = _A_QM + M_WIDTH


def _in_proj_kernel(x_ref, eg_ref, eb_ref, wa_ref, wbT_ref, gg_ref, gb_ref, ws_ref, bs_ref,
                    kmT_ref, vm_ref,
                    h0_ref, yg_ref, ym_ref, qT_ref, k_ref, vT_ref):
    rows = x_ref.shape[0]
    h = _layer_norm(x_ref[...], eg_ref[...], eb_ref[...])
    h0_ref[...] = h
    hb = h.astype(BF16)

    pa = _dot(hb, wa_ref[...])
    pbT = lax.dot_general(wbT_ref[...], hb, NT_DIMS, preferred_element_type=F32)
    qT_ref[...] = (pbT[:DA_QK] * (DA_HEAD_DIM ** -0.5 * LOG2E)).astype(BF16)
    vT_ref[...] = pbT[DA_QK:].astype(BF16)
    k_ref[...] = pa[:, _A_K:_A_QM].astype(BF16)

    u = _gelu(pa[:, _A_U:_A_V])
    v = _layer_norm(_gelu(pa[:, _A_V:_A_K]), gg_ref[...], gb_ref[...]).astype(BF16)
    group_of_lane = lax.broadcasted_iota(jnp.int32, (CHUNK, G_WIDTH), 1) // G_DIM
    ws = ws_ref[...]
    bs = bs_ref[...]
    for c in range(rows // CHUNK):
        sl = slice(c * CHUNK, (c + 1) * CHUNK)
        r = _dot(ws, v[sl])
        s = r[(G_HEADS - 1) * CHUNK:]
        for g in range(G_HEADS - 2, -1, -1):
            s = jnp.where(group_of_lane == g, r[g * CHUNK:(g + 1) * CHUNK], s)
        yg_ref[sl, :] = (u[sl] * (s + bs)).astype(BF16)

    qm = pa[:, _A_QM:_A_END].astype(BF16)
    o = jnp.zeros((rows, M_WIDTH), F32)
    for hd in range(M_HEADS):
        s = _dot(qm, kmT_ref[hd])
        p = jnp.exp(s - jnp.max(s, axis=-1, keepdims=True))
        p = p / jnp.sum(p, axis=-1, keepdims=True)
        o = o + _dot(p.astype(BF16), vm_ref[hd])
    ym_ref[...] = o.astype(BF16)


def _in_proj(x, eg, eb, w_a, w_bT, gg, gb, ws, bs_tile, kmT, vm):
    B, T, D = x.shape
    R = PROJ_ROWS
    const2 = lambda shape: pl.BlockSpec(shape, lambda b, i: (0, 0))
    return pl.pallas_call(
        _in_proj_kernel,
        grid=(B, T // R),
        in_specs=[
            pl.BlockSpec((None, R, D), lambda b, i: (b, i, 0)),
            const2((1, D)), const2((1, D)),
            const2((D, _A_END)),
            const2((DA_QK + DA_WIDTH, D)),
            const2((1, G_WIDTH)), const2((1, G_WIDTH)),
            const2((G_HEADS * CHUNK, CHUNK)),
            const2((CHUNK, G_WIDTH)),
            pl.BlockSpec((None, M_HEADS, M_WIDTH, MEM_LEN), lambda b, i: (b, 0, 0, 0)),
            pl.BlockSpec((None, M_HEADS, MEM_LEN, M_WIDTH), lambda b, i: (b, 0, 0, 0)),
        ],
        out_specs=[
            pl.BlockSpec((None, R, D), lambda b, i: (b, i, 0)),
            pl.BlockSpec((None, R, G_WIDTH), lambda b, i: (b, i, 0)),
            pl.BlockSpec((None, R, M_WIDTH), lambda b, i: (b, i, 0)),
            pl.BlockSpec((None, DA_QK, R), lambda b, i: (b, 0, i)),
            pl.BlockSpec((None, R, DA_QK), lambda b, i: (b, i, 0)),
            pl.BlockSpec((None, DA_WIDTH, R), lambda b, i: (b, 0, i)),
        ],
        out_shape=[
            jax.ShapeDtypeStruct((B, T, D), F32),
            jax.ShapeDtypeStruct((B, T, G_WIDTH), BF16),
            jax.ShapeDtypeStruct((B, T, M_WIDTH), BF16),
            jax.ShapeDtypeStruct((B, DA_QK, T), BF16),
            jax.ShapeDtypeStruct((B, T, DA_QK), BF16),
            jax.ShapeDtypeStruct((B, DA_WIDTH, T), BF16),
        ],
        compiler_params=_params(("arbitrary", "arbitrary")),
        name="in_proj",
    )(x, eg, eb, w_a, w_bT, gg, gb, ws, bs_tile, kmT, vm)


_POS_SPLIT = 64
_BIAS_ROWS = BF16_SUBLANE_TILE
_ACC_ROWS = DA_V_DIM + BF16_SUBLANE_TILE


def _bf16_split3(x):
    a = x.astype(BF16).astype(F32)
    b = (x - a).astype(BF16).astype(F32)
    c = x - a - b
    return a, b, c


def _diff_attn_kernel(slopes_ref, qT_ref, k_ref, pos_ref, vT_ref, lq1_ref, lk1_ref, lq2_ref,
                      lk2_ref, g_ref, o_ref, qa_ref, base_ref, sa_ref, sb_ref, bma_ref, bmb_ref,
                      m_ref, acc_ref, diag_ref, *, lambda_init):
    tq = ATT_Q
    tk = ATT_K
    two_dk = 2 * DA_HEAD_DIM
    nk = k_ref.shape[0] // tk
    nq = qT_ref.shape[1] // tq
    sub = tq // tk
    c = slopes_ref[pl.program_id(1)] * LOG2E

    kk = lax.broadcasted_iota(jnp.int32, (tk, tk), 0)
    qq = lax.broadcasted_iota(jnp.int32, (tk, tk), 1)
    diag_ref[...] = -c * jnp.abs(kk - qq).astype(F32)

    lane = lax.broadcasted_iota(jnp.int32, (1, 2 * tq), 1)
    col = jnp.where(lane >= tq, lane - tq, lane)
    group = col // tk

    lam = (jnp.exp(jnp.sum(lq1_ref[...] * lk1_ref[...], axis=-1, keepdims=True))
           - jnp.exp(jnp.sum(lq2_ref[...] * lk2_ref[...], axis=-1, keepdims=True))
           + lambda_init)
    ones_row = jnp.where(lax.broadcasted_iota(jnp.int32, (BF16_SUBLANE_TILE, tk), 0) == 0,
                         1.0, 0.0).astype(BF16)
    qa_ref[two_dk + _BIAS_ROWS:, :] = jnp.zeros((two_dk - _BIAS_ROWS, 2 * tq), BF16)

    def load_queries(qn):
        qt = qT_ref[:, pl.ds(pl.multiple_of(qn * tq, tq), tq)]
        zero = jnp.zeros((DA_HEAD_DIM, tq), BF16)
        qa_ref[:DA_HEAD_DIM, :] = jnp.concatenate([qt[:DA_HEAD_DIM], zero], axis=1)
        qa_ref[DA_HEAD_DIM:two_dk, :] = jnp.concatenate([zero, qt[DA_HEAD_DIM:]], axis=1)
        qpos = (qn * tq + col).astype(F32)
        c1, c2, c3 = _bf16_split3(jnp.full((1, 2 * tq), c, F32))
        v1, v2, v3 = _bf16_split3(c * qpos)
        r = lax.broadcasted_iota(jnp.int32, (_BIAS_ROWS, 2 * tq), 0)
        base = jnp.zeros((_BIAS_ROWS, 2 * tq), F32)
        for idx, val in enumerate((-c1, -c2, -c3, -c1, -c2, -c3, v1, v2, v3)):
            base = jnp.where(r == idx, val, base)
        base_ref[...] = base

    def block_start(qn, t):
        return pl.multiple_of(lax.rem(qn * sub + t, nk) * tk, tk)

    n_strips = 2 * sub

    def step(score_args, consume_args):
        if score_args is not None:
            qn, t, s_ref, bm_ref, straddles = score_args
            start = block_start(qn, t)
            if straddles:
                sign = jnp.where(group < t, 1.0, jnp.where(group > t, -1.0, 0.0)).astype(F32)
            else:
                sign = jnp.where(start > qn * tq, 1.0, -1.0).astype(F32)
            qa_ref[two_dk:two_dk + _BIAS_ROWS, :] = (sign * base_ref[...]).astype(BF16)
            kb = jnp.concatenate([k_ref[pl.ds(start, tk), :], pos_ref[pl.ds(start, tk), :]],
                                 axis=1)
        if consume_args is not None:
            cqn, ct, cs_ref, cbm_ref = consume_args
            vb = jnp.concatenate([vT_ref[:, pl.ds(block_start(cqn, ct), tk)], ones_row], axis=0)
        for j in range(n_strips):
            cols = slice(j * tk, (j + 1) * tk)
            if score_args is not None:
                s = _dot(kb, qa_ref[:, cols])
                if straddles and j % sub == t:
                    s = s + diag_ref[...]
                s_ref[:, cols] = s
                bm_ref[:, cols] = jnp.max(s, axis=0, keepdims=True)
            if consume_args is not None:
                m_old = m_ref[:, cols]
                m_new = jnp.maximum(m_old, cbm_ref[:, cols])
                alpha = jnp.exp2(m_old - m_new)
                m_ref[:, cols] = m_new
                p = jnp.exp2(cs_ref[:, cols] - m_new).astype(BF16)
                acc_ref[:, cols] = alpha * acc_ref[:, cols] + _dot(vb, p)

    def reset_stats():
        acc_ref[...] = jnp.zeros_like(acc_ref)
        m_ref[...] = jnp.full(m_ref.shape, -1e30, F32)

    def finish(qn):
        inv_l = 1.0 / acc_ref[DA_V_DIM:DA_V_DIM + 1, :]
        oT = (acc_ref[:DA_V_DIM, :tq] * inv_l[:, :tq]
              - lam * (acc_ref[:DA_V_DIM, tq:] * inv_l[:, tq:]))
        o = oT.T
        ms = jnp.mean(o * o, axis=-1, keepdims=True)
        o = o * lax.rsqrt(ms + LN_EPS) * g_ref[...] * (1.0 - lambda_init)
        o_ref[pl.ds(pl.multiple_of(qn * tq, tq), tq), :] = o.astype(o_ref.dtype)

    bufs = ((sa_ref, bma_ref), (sb_ref, bmb_ref))

    last = (nk - 1) % 2

    def head_rest(qn):
        for t in range(1, sub):
            step((qn, t) + bufs[t % 2] + (True,), (qn, t - 1) + bufs[(t - 1) % 2])

    def body(qn):
        def pair(i, inner):
            t = sub + 2 * i
            step((qn, t) + bufs[sub % 2] + (False,), (qn, t - 1) + bufs[(sub - 1) % 2])
            step((qn, t + 1) + bufs[(sub + 1) % 2] + (False,), (qn, t) + bufs[sub % 2])
            return inner

        for i in range((nk - sub) // 2):
            pair(i, 0)

    load_queries(0)
    step((0, 0) + bufs[0] + (True,), None)
    reset_stats()
    head_rest(0)

    def query_block(qn, carry):
        body(qn)
        load_queries(qn + 1)
        step((qn + 1, 0) + bufs[0] + (True,), (qn, nk - 1) + bufs[last])
        finish(qn)
        reset_stats()
        head_rest(qn + 1)
        return carry

    lax.fori_loop(0, nq - 1, query_block, 0)
    body(nq - 1)
    step(None, (nq - 1, nk - 1) + bufs[last])
    finish(nq - 1)


def _key_positions(T):
    t = jnp.arange(T, dtype=jnp.int32)
    hi = ((t // _POS_SPLIT) * _POS_SPLIT).astype(F32)
    lo = (t % _POS_SPLIT).astype(F32)
    one = jnp.ones((T,), F32)
    cols = jnp.stack([hi, hi, hi, lo, lo, lo, one, one, one], axis=1)
    return jnp.pad(cols, ((0, 0), (0, 2 * DA_HEAD_DIM - cols.shape[1]))).astype(BF16)


def _diff_attn(slopes, qT, k, vT, lq1, lk1, lq2, lk2, g, lambda_init):
    B, _, T = qT.shape
    two_dk = 2 * DA_HEAD_DIM
    assert ATT_Q % ATT_K == 0 and T % ATT_Q == 0 and T // ATT_Q >= 2
    assert (T // ATT_K - ATT_Q // ATT_K) % 2 == 0
    vec = lambda: pl.BlockSpec((1, DA_HEAD_DIM), lambda b, h: (0, 0))
    return pl.pallas_call(
        functools.partial(_diff_attn_kernel, lambda_init=lambda_init),
        grid=(B, DA_HEADS),
        in_specs=[
            pl.BlockSpec(memory_space=pltpu.SMEM),
            pl.BlockSpec((None, two_dk, T), lambda b, h: (b, h, 0)),
            pl.BlockSpec((None, T, two_dk), lambda b, h: (b, 0, h)),
            pl.BlockSpec((T, two_dk), lambda b, h: (0, 0)),
            pl.BlockSpec((None, DA_V_DIM, T), lambda b, h: (b, h, 0)),
            vec(), vec(), vec(), vec(),
            pl.BlockSpec((1, DA_V_DIM), lambda b, h: (0, 0)),
        ],
        out_specs=pl.BlockSpec((None, T, DA_V_DIM), lambda b, h: (b, 0, h)),
        out_shape=jax.ShapeDtypeStruct((B, T, DA_WIDTH), BF16),
        scratch_shapes=[
            pltpu.VMEM((2 * two_dk, 2 * ATT_Q), BF16),
            pltpu.VMEM((_BIAS_ROWS, 2 * ATT_Q), F32),
            pltpu.VMEM((ATT_K, 2 * ATT_Q), F32),
            pltpu.VMEM((ATT_K, 2 * ATT_Q), F32),
            pltpu.VMEM((1, 2 * ATT_Q), F32),
            pltpu.VMEM((1, 2 * ATT_Q), F32),
            pltpu.VMEM((1, 2 * ATT_Q), F32),
            pltpu.VMEM((_ACC_ROWS, 2 * ATT_Q), F32),
            pltpu.VMEM((ATT_K, ATT_K), F32),
        ],
        compiler_params=_params(("arbitrary", "arbitrary")),
        name="diff_attn",
    )(slopes, qT, k, _key_positions(T), vT, lq1, lk1, lq2, lk2, g)


def _out_proj_kernel(h0_ref, yg_ref, yd_ref, ym_ref, w_ref, g_ref, b_ref, h1_ref):
    y = (_dot(yg_ref[...], w_ref[:G_WIDTH])
         + _dot(yd_ref[...], w_ref[G_WIDTH:G_WIDTH + DA_WIDTH])
         + _dot(ym_ref[...], w_ref[G_WIDTH + DA_WIDTH:]))
    h1_ref[...] = _layer_norm(ALPHA * h0_ref[...] + y, g_ref[...], b_ref[...])


def _out_proj(h0, yg, yd, ym, w_out, g, b):
    N, D = h0.shape
    R = PROJ_ROWS
    rows = lambda width: pl.BlockSpec((R, width), lambda i: (i, 0))
    const = lambda shape: pl.BlockSpec(shape, lambda i: (0, 0))
    return pl.pallas_call(
        _out_proj_kernel,
        grid=(N // R,),
        in_specs=[rows(D), rows(G_WIDTH), rows(DA_WIDTH), rows(M_WIDTH),
                  const((D, D)), const((1, D)), const((1, D))],
        out_specs=rows(D),
        out_shape=jax.ShapeDtypeStruct((N, D), F32),
        compiler_params=_params(("arbitrary",)),
        name="out_proj",
    )(h0, yg, yd, ym, w_out, g, b)


_FFN_SPLIT = 2
_FFN_UNROLL = 8


def _conv_ffn_kernel(h1_ref, prev_ref, next_ref, wup_ref, cw_ref, cb_ref, wdn_ref,
                     g_ref, b_ref, o_ref, hext_ref, a0_ref, a1_ref, f_ref, *, steps_per_seq):
    rows = h1_ref.shape[0]
    n_chunks = wdn_ref.shape[0] // FFN_COLS
    i = pl.program_id(0) % steps_per_seq
    prev = prev_ref[...].astype(BF16)
    nxt = next_ref[...].astype(BF16)
    hext_ref[:HALO] = jnp.where(i == 0, jnp.zeros_like(prev), prev)
    hext_ref[HALO:HALO + rows] = h1_ref[...].astype(BF16)
    hext_ref[HALO + rows:] = jnp.where(i == steps_per_seq - 1, jnp.zeros_like(nxt), nxt)

    def columns(c):
        gate = pl.ds(pl.multiple_of(c * FFN_COLS, FFN_COLS), FFN_COLS)
        val = pl.ds(pl.multiple_of(D_FF + c * FFN_COLS, FFN_COLS), FFN_COLS)
        return gate, val

    def stage(up, down, first=False):
        ext_rows = rows + 2 * HALO
        up_rows = ext_rows // _FFN_SPLIT
        down_rows = rows // (2 * _FFN_SPLIT)
        if up is not None:
            up_cols = columns(up[0])

            def project(piece):
                half, part = divmod(piece, _FFN_SPLIT)
                r0 = part * up_rows
                up[1][half, r0:r0 + up_rows, :] = _dot(hext_ref[r0:r0 + up_rows, :],
                                                       wup_ref[:, up_cols[half]])

        if down is not None:
            c, a_ref = down
            down_cols = columns(c)
            w = wdn_ref[pl.ds(pl.multiple_of(c * FFN_COLS, FFN_COLS), FFN_COLS), :]

            def conv(half, r0):
                cols = down_cols[half]
                cw = cw_ref[:, cols]
                lo = HALO + r0
                return (cw[0:1] * a_ref[half, lo - 1:lo - 1 + down_rows, :]
                        + cw[1:2] * a_ref[half, lo:lo + down_rows, :]
                        + cw[2:3] * a_ref[half, lo + 1:lo + 1 + down_rows, :]
                        + cb_ref[:, cols])

            def contract(piece):
                r0 = piece * down_rows
                act = (_gelu(conv(0, r0)) * conv(1, r0)).astype(BF16)
                if first:
                    f_ref[r0:r0 + down_rows, :] = _dot(act, w)
                else:
                    f_ref[r0:r0 + down_rows, :] += _dot(act, w)

        for piece in range(2 * _FFN_SPLIT):
            if up is not None:
                project(piece)
            if down is not None:
                contract(piece)

    stage((0, a0_ref), None)
    stage((1, a1_ref), (0, a0_ref), first=True)

    def group(p, carry):
        c = _FFN_UNROLL * p + 1
        for u in range(0, _FFN_UNROLL, 2):
            stage((c + u + 1, a0_ref), (c + u, a1_ref))
            stage((c + u + 2, a1_ref), (c + u + 1, a0_ref))
        return carry

    lax.fori_loop(0, (n_chunks - 3) // _FFN_UNROLL, group, 0)
    stage((n_chunks - 1, a0_ref), (n_chunks - 2, a1_ref))
    stage(None, (n_chunks - 1, a0_ref))
    o_ref[...] = _layer_norm(ALPHA * h1_ref[...] + f_ref[...], g_ref[...], b_ref[...])


def _conv_ffn(h1, w_up, conv_w, conv_b, w_down, g, b, seq_len):
    N, D = h1.shape
    R = FFN_ROWS
    halo_per_step = R // HALO
    n_halo = N // HALO
    n_chunks = D_FF // FFN_COLS
    assert n_chunks >= 3 and n_chunks % 2 == 1
    const = lambda shape: pl.BlockSpec(shape, lambda i: (0, 0))
    return pl.pallas_call(
        functools.partial(_conv_ffn_kernel, steps_per_seq=seq_len // R),
        grid=(N // R,),
        in_specs=[
            pl.BlockSpec((R, D), lambda i: (i, 0)),
            pl.BlockSpec((HALO, D), lambda i: (jnp.maximum(i * halo_per_step - 1, 0), 0)),
            pl.BlockSpec((HALO, D), lambda i: (jnp.minimum((i + 1) * halo_per_step, n_halo - 1), 0)),
            const((D, 2 * D_FF)), const((CONV_W, 2 * D_FF)), const((1, 2 * D_FF)),
            const((D_FF, D)), const((1, D)), const((1, D)),
        ],
        out_specs=pl.BlockSpec((R, D), lambda i: (i, 0)),
        out_shape=jax.ShapeDtypeStruct((N, D), F32),
        scratch_shapes=[
            pltpu.VMEM((R + 2 * HALO, D), BF16),
            pltpu.VMEM((2, R + 2 * HALO, FFN_COLS), F32),
            pltpu.VMEM((2, R + 2 * HALO, FFN_COLS), F32),
            pltpu.VMEM((R, D), F32),
        ],
        compiler_params=_params(("arbitrary",)),
        name="conv_ffn",
    )(h1, h1, h1, w_up, conv_w, conv_b, w_down, g, b)


def _alibi_slopes(n):
    return jnp.asarray([2.0 ** (-8.0 * (i + 1) / n) for i in range(n)], F32)


def kernel(x, mem, ln_emb_g, ln_emb_b, w_in, gmlp_ln_g, gmlp_ln_b, gmlp_ws, gmlp_bs, lambda_q1, lambda_k1, lambda_q2, lambda_k2, da_subln_g, mem_ln_g, mem_ln_b, w_mem_kv, w_out, ln1_g, ln1_b, w_up, conv_w, conv_b, w_down, ln2_g, ln2_b):
    B, T, D = x.shape
    assert DEPTH == 1 and w_in.shape[0] == 1
    assert T % PROJ_ROWS == 0 and T % ATT_Q == 0 and T % ATT_K == 0 and T % FFN_ROWS == 0
    assert D_FF % FFN_COLS == 0
    l = 0
    lambda_init = 0.8 - 0.6 * math.exp(-0.3 * l)
    row = lambda a: a.reshape(1, -1)

    w = w_in[l]
    u_end, v_end = G_WIDTH, 2 * G_WIDTH
    q_end = v_end + DA_QK
    k_end = q_end + DA_QK
    vd_end = k_end + DA_WIDTH
    w_a = jnp.concatenate([w[:, :v_end], w[:, q_end:k_end], w[:, vd_end:]], axis=1).astype(BF16)
    w_bT = jnp.concatenate([w[:, v_end:q_end], w[:, k_end:vd_end]], axis=1).T.astype(BF16)
    w_kmT = w_mem_kv[l][:, :M_WIDTH].T.astype(BF16)
    w_vm = w_mem_kv[l][:, M_WIDTH:].astype(BF16)
    ws = gmlp_ws[l].reshape(G_HEADS * CHUNK, CHUNK).astype(BF16)
    bs_tile = jnp.repeat(gmlp_bs[l].T, G_DIM, axis=1)

    kmT, vm = _mem_kv(mem, row(mem_ln_g[l]), row(mem_ln_b[l]), w_kmT, w_vm)
    h0, yg, ym, qT, k, vT = _in_proj(
        x, row(ln_emb_g), row(ln_emb_b), w_a, w_bT, row(gmlp_ln_g[l]), row(gmlp_ln_b[l]),
        ws, bs_tile, kmT, vm)
    yd = _diff_attn(_alibi_slopes(DA_HEADS), qT, k, vT, row(lambda_q1[l]), row(lambda_k1[l]),
                    row(lambda_q2[l]), row(lambda_k2[l]), row(da_subln_g[l]), lambda_init)

    N = B * T
    h1 = _out_proj(h0.reshape(N, D), yg.reshape(N, G_WIDTH), yd.reshape(N, DA_WIDTH),
                   ym.reshape(N, M_WIDTH), w_out[l].astype(BF16), row(ln1_g[l]), row(ln1_b[l]))
    out = _conv_ffn(h1, w_up[l].astype(BF16), conv_w[l], row(conv_b[l]),
                    w_down[l].astype(BF16), row(ln2_g[l]), row(ln2_b[l]), T)
    return out.reshape(B, T, D)
```

```python
import functools
import math

import jax
import jax.numpy as jnp
from jax import lax
from jax.experimental import pallas as pl
from jax.experimental.pallas import tpu as pltpu

F32 = jnp.float32
BF16 = jnp.bfloat16

D_MODEL = 1024
DEPTH = 1
MEM_LEN = 256
CHUNK = 128
G_HEADS = 4
G_WIDTH = D_MODEL // 4
G_DIM = G_WIDTH // G_HEADS
DA_HEADS = 4
DA_WIDTH = D_MODEL // 2
DA_V_DIM = DA_WIDTH // DA_HEADS
DA_HEAD_DIM = DA_V_DIM // 2
DA_QK = DA_HEADS * 2 * DA_HEAD_DIM
M_HEADS = 4
M_WIDTH = D_MODEL // 4
M_HEAD_DIM = M_WIDTH // M_HEADS
D_FF = 2816
CONV_W = 3
LN_EPS = 1e-5
ALPHA = (2.0 * DEPTH) ** 0.25
LOG2E = math.log2(math.e)

V7X_VMEM_LIMIT_BYTES = 56 * 1024 * 1024
BF16_SUBLANE_TILE = 16

PROJ_ROWS = 512
ATT_Q = 1024
ATT_K = 512
FFN_ROWS = 512
FFN_COLS = 256
HALO = BF16_SUBLANE_TILE

NT_DIMS = (((1,), (1,)), ((), ()))


def _layer_norm(x, g, b):
    mu = jnp.mean(x, axis=-1, keepdims=True)
    xc = x - mu
    var = jnp.mean(xc * xc, axis=-1, keepdims=True)
    return xc * lax.rsqrt(var + LN_EPS) * g + b


def _gelu(x):
    return 0.5 * x * (1.0 + lax.erf(x * (2.0 ** -0.5)))


def _dot(a, b):
    return jnp.dot(a, b, preferred_element_type=F32)


def _params(semantics):
    return pltpu.CompilerParams(dimension_semantics=semantics,
                                vmem_limit_bytes=V7X_VMEM_LIMIT_BYTES)


def _mem_kv_kernel(mem_ref, g_ref, b_ref, wkT_ref, wv_ref, kmT_ref, vm_ref):
    m = _layer_norm(mem_ref[...], g_ref[...], b_ref[...]).astype(BF16)
    kT = lax.dot_general(wkT_ref[...], m, NT_DIMS, preferred_element_type=F32)
    kT = kT * (M_HEAD_DIM ** -0.5)
    v = _dot(m, wv_ref[...])
    head_of_row = lax.broadcasted_iota(jnp.int32, (M_WIDTH, MEM_LEN), 0) // M_HEAD_DIM
    head_of_col = lax.broadcasted_iota(jnp.int32, (MEM_LEN, M_WIDTH), 1) // M_HEAD_DIM
    for h in range(M_HEADS):
        kmT_ref[h] = jnp.where(head_of_row == h, kT, 0.0).astype(BF16)
        vm_ref[h] = jnp.where(head_of_col == h, v, 0.0).astype(BF16)


def _mem_kv(mem, g, b, wkT, wv):
    B = mem.shape[0]
    row = lambda: pl.BlockSpec((1, D_MODEL), lambda i: (0, 0))
    return pl.pallas_call(
        _mem_kv_kernel,
        grid=(B,),
        in_specs=[
            pl.BlockSpec((None, MEM_LEN, D_MODEL), lambda i: (i, 0, 0)),
            row(), row(),
            pl.BlockSpec((M_WIDTH, D_MODEL), lambda i: (0, 0)),
            pl.BlockSpec((D_MODEL, M_WIDTH), lambda i: (0, 0)),
        ],
        out_specs=[
            pl.BlockSpec((None, M_HEADS, M_WIDTH, MEM_LEN), lambda i: (i, 0, 0, 0)),
            pl.BlockSpec((None, M_HEADS, MEM_LEN, M_WIDTH), lambda i: (i, 0, 0, 0)),
        ],
        out_shape=[
            jax.ShapeDtypeStruct((B, M_HEADS, M_WIDTH, MEM_LEN), BF16),
            jax.ShapeDtypeStruct((B, M_HEADS, MEM_LEN, M_WIDTH), BF16),
        ],
        compiler_params=_params(("arbitrary",)),
        name="mem_kv",
    )(mem, g, b, wkT, wv)


_A_U = 0
_A_V = G_WIDTH
_A_K = 2 * G_WIDTH
_A_QM = 2 * G_WIDTH + DA_QK
_A_END = _A_QM + M_WIDTH


def _in_proj_kernel(x_ref, eg_ref, eb_ref, wa_ref, wbT_ref, gg_ref, gb_ref, ws_ref, bs_ref,
                    kmT_ref, vm_ref,
                    h0_ref, yg_ref, ym_ref, qT_ref, k_ref, vT_ref):
    rows = x_ref.shape[0]
    h = _layer_norm(x_ref[...], eg_ref[...], eb_ref[...])
    h0_ref[...] = h
    hb = h.astype(BF16)

    pa = _dot(hb, wa_ref[...])
    pbT = lax.dot_general(wbT_ref[...], hb, NT_DIMS, preferred_element_type=F32)
    qT_ref[...] = (pbT[:DA_QK] * (DA_HEAD_DIM ** -0.5 * LOG2E)).astype(BF16)
    vT_ref[...] = pbT[DA_QK:].astype(BF16)
    k_ref[...] = pa[:, _A_K:_A_QM].astype(BF16)

    u = _gelu(pa[:, _A_U:_A_V])
    v = _layer_norm(_gelu(pa[:, _A_V:_A_K]), gg_ref[...], gb_ref[...]).astype(BF16)
    group_of_lane = lax.broadcasted_iota(jnp.int32, (CHUNK, G_WIDTH), 1) // G_DIM
    ws = ws_ref[...]
    bs = bs_ref[...]
    for c in range(rows // CHUNK):
        sl = slice(c * CHUNK, (c + 1) * CHUNK)
        r = _dot(ws, v[sl])
        s = r[(G_HEADS - 1) * CHUNK:]
        for g in range(G_HEADS - 2, -1, -1):
            s = jnp.where(group_of_lane == g, r[g * CHUNK:(g + 1) * CHUNK], s)
        yg_ref[sl, :] = (u[sl] * (s + bs)).astype(BF16)

    qm = pa[:, _A_QM:_A_END].astype(BF16)
    o = jnp.zeros((rows, M_WIDTH), F32)
    for hd in range(M_HEADS):
        s = _dot(qm, kmT_ref[hd])
        p = jnp.exp(s - jnp.max(s, axis=-1, keepdims=True))
        p = p / jnp.sum(p, axis=-1, keepdims=True)
        o = o + _dot(p.astype(BF16), vm_ref[hd])
    ym_ref[...] = o.astype(BF16)


def _in_proj(x, eg, eb, w_a, w_bT, gg, gb, ws, bs_tile, kmT, vm):
    B, T, D = x.shape
    R = PROJ_ROWS
    const2 = lambda shape: pl.BlockSpec(shape, lambda b, i: (0, 0))
    return pl.pallas_call(
        _in_proj_kernel,
        grid=(B, T // R),
        in_specs=[
            pl.BlockSpec((None, R, D), lambda b, i: (b, i, 0)),
            const2((1, D)), const2((1, D)),
            const2((D, _A_END)),
            const2((DA_QK + DA_WIDTH, D)),
            const2((1, G_WIDTH)), const2((1, G_WIDTH)),
            const2((G_HEADS * CHUNK, CHUNK)),
            const2((CHUNK, G_WIDTH)),
            pl.BlockSpec((None, M_HEADS, M_WIDTH, MEM_LEN), lambda b, i: (b, 0, 0, 0)),
            pl.BlockSpec((None, M_HEADS, MEM_LEN, M_WIDTH), lambda b, i: (b, 0, 0, 0)),
        ],
        out_specs=[
            pl.BlockSpec((None, R, D), lambda b, i: (b, i, 0)),
            pl.BlockSpec((None, R, G_WIDTH), lambda b, i: (b, i, 0)),
            pl.BlockSpec((None, R, M_WIDTH), lambda b, i: (b, i, 0)),
            pl.BlockSpec((None, DA_QK, R), lambda b, i: (b, 0, i)),
            pl.BlockSpec((None, R, DA_QK), lambda b, i: (b, i, 0)),
            pl.BlockSpec((None, DA_WIDTH, R), lambda b, i: (b, 0, i)),
        ],
        out_shape=[
            jax.ShapeDtypeStruct((B, T, D), F32),
            jax.ShapeDtypeStruct((B, T, G_WIDTH), BF16),
            jax.ShapeDtypeStruct((B, T, M_WIDTH), BF16),
            jax.ShapeDtypeStruct((B, DA_QK, T), BF16),
            jax.ShapeDtypeStruct((B, T, DA_QK), BF16),
            jax.ShapeDtypeStruct((B, DA_WIDTH, T), BF16),
        ],
        compiler_params=_params(("arbitrary", "arbitrary")),
        name="in_proj",
    )(x, eg, eb, w_a, w_bT, gg, gb, ws, bs_tile, kmT, vm)


_POS_SPLIT = 64
_BIAS_ROWS = BF16_SUBLANE_TILE
_ACC_ROWS = DA_V_DIM + BF16_SUBLANE_TILE


def _bf16_split3(x):
    a = x.astype(BF16).astype(F32)
    b = (x - a).astype(BF16).astype(F32)
    c = x - a - b
    return a, b, c


def _diff_attn_kernel(slopes_ref, qT_ref, k_ref, pos_ref, vT_ref, lq1_ref, lk1_ref, lq2_ref,
                      lk2_ref, g_ref, o_ref, qa_ref, base_ref, sa_ref, sb_ref, bma_ref, bmb_ref,
                      m_ref, acc_ref, diag_ref, *, lambda_init):
    tq = ATT_Q
    tk = ATT_K
    two_dk = 2 * DA_HEAD_DIM
    nk = k_ref.shape[0] // tk
    nq = qT_ref.shape[1] // tq
    sub = tq // tk
    c = slopes_ref[pl.program_id(1)] * LOG2E

    kk = lax.broadcasted_iota(jnp.int32, (tk, tk), 0)
    qq = lax.broadcasted_iota(jnp.int32, (tk, tk), 1)
    diag_ref[...] = -c * jnp.abs(kk - qq).astype(F32)

    lane = lax.broadcasted_iota(jnp.int32, (1, 2 * tq), 1)
    col = jnp.where(lane >= tq, lane - tq, lane)
    group = col // tk

    lam = (jnp.exp(jnp.sum(lq1_ref[...] * lk1_ref[...], axis=-1, keepdims=True))
           - jnp.exp(jnp.sum(lq2_ref[...] * lk2_ref[...], axis=-1, keepdims=True))
           + lambda_init)
    ones_row = jnp.where(lax.broadcasted_iota(jnp.int32, (BF16_SUBLANE_TILE, tk), 0) == 0,
                         1.0, 0.0).astype(BF16)
    qa_ref[two_dk + _BIAS_ROWS:, :] = jnp.zeros((two_dk - _BIAS_ROWS, 2 * tq), BF16)

    def load_queries(qn):
        qt = qT_ref[:, pl.ds(pl.multiple_of(qn * tq, tq), tq)]
        zero = jnp.zeros((DA_HEAD_DIM, tq), BF16)
        qa_ref[:DA_HEAD_DIM, :] = jnp.concatenate([qt[:DA_HEAD_DIM], zero], axis=1)
        qa_ref[DA_HEAD_DIM:two_dk, :] = jnp.concatenate([zero, qt[DA_HEAD_DIM:]], axis=1)
        qpos = (qn * tq + col).astype(F32)
        c1, c2, c3 = _bf16_split3(jnp.full((1, 2 * tq), c, F32))
        v1, v2, v3 = _bf16_split3(c * qpos)
        r = lax.broadcasted_iota(jnp.int32, (_BIAS_ROWS, 2 * tq), 0)
        base = jnp.zeros((_BIAS_ROWS, 2 * tq), F32)
        for idx, val in enumerate((-c1, -c2, -c3, -c1, -c2, -c3, v1, v2, v3)):
            base = jnp.where(r == idx, val, base)
        base_ref[...] = base

    def block_start(qn, t):
        return pl.multiple_of(lax.rem(qn * sub + t, nk) * tk, tk)

    n_strips = 2 * sub

    def step(score_args, consume_args):
        if score_args is not None:
            qn, t, s_ref, bm_ref, straddles = score_args
            start = block_start(qn, t)
            if straddles:
                sign = jnp.where(group < t, 1.0, jnp.where(group > t, -1.0, 0.0)).astype(F32)
            else:
                sign = jnp.where(start > qn * tq, 1.0, -1.0).astype(F32)
            qa_ref[two_dk:two_dk + _BIAS_ROWS, :] = (sign * base_ref[...]).astype(BF16)
            kb = jnp.concatenate([k_ref[pl.ds(start, tk), :], pos_ref[pl.ds(start, tk), :]],
                                 axis=1)
        if consume_args is not None:
            cqn, ct, cs_ref, cbm_ref = consume_args
            vb = jnp.concatenate([vT_ref[:, pl.ds(block_start(cqn, ct), tk)], ones_row], axis=0)
        for j in range(n_strips):
            cols = slice(j * tk, (j + 1) * tk)
            if score_args is not None:
                s = _dot(kb, qa_ref[:, cols])
                if straddles and j % sub == t:
                    s = s + diag_ref[...]
                s_ref[:, cols] = s
                bm_ref[:, cols] = jnp.max(s, axis=0, keepdims=True)
            if consume_args is not None:
                m_old = m_ref[:, cols]
                m_new = jnp.maximum(m_old, cbm_ref[:, cols])
                alpha = jnp.exp2(m_old - m_new)
                m_ref[:, cols] = m_new
                p = jnp.exp2(cs_ref[:, cols] - m_new).astype(BF16)
                acc_ref[:, cols] = alpha * acc_ref[:, cols] + _dot(vb, p)

    def reset_stats():
        acc_ref[...] = jnp.zeros_like(acc_ref)
        m_ref[...] = jnp.full(m_ref.shape, -1e30, F32)

    def finish(qn):
        inv_l = 1.0 / acc_ref[DA_V_DIM:DA_V_DIM + 1, :]
        oT = (acc_ref[:DA_V_DIM, :tq] * inv_l[:, :tq]
              - lam * (acc_ref[:DA_V_DIM, tq:] * inv_l[:, tq:]))
        o = oT.T
        ms = jnp.mean(o * o, axis=-1, keepdims=True)
        o = o * lax.rsqrt(ms + LN_EPS) * g_ref[...] * (1.0 - lambda_init)
        o_ref[pl.ds(pl.multiple_of(qn * tq, tq), tq), :] = o.astype(o_ref.dtype)

    bufs = ((sa_ref, bma_ref), (sb_ref, bmb_ref))

    last = (nk - 1) % 2

    def head_rest(qn):
        for t in range(1, sub):
            step((qn, t) + bufs[t % 2] + (True,), (qn, t - 1) + bufs[(t - 1) % 2])

    def body(qn):
        def pair(i, inner):
            t = sub + 2 * i
            step((qn, t) + bufs[sub % 2] + (False,), (qn, t - 1) + bufs[(sub - 1) % 2])
            step((qn, t + 1) + bufs[(sub + 1) % 2] + (False,), (qn, t) + bufs[sub % 2])
            return inner

        for i in range((nk - sub) // 2):
            pair(i, 0)

    load_queries(0)
    step((0, 0) + bufs[0] + (True,), None)
    reset_stats()
    head_rest(0)

    def query_block(qn, carry):
        body(qn)
        load_queries(qn + 1)
        step((qn + 1, 0) + bufs[0] + (True,), (qn, nk - 1) + bufs[last])
        finish(qn)
        reset_stats()
        head_rest(qn + 1)
        return carry

    lax.fori_loop(0, nq - 1, query_block, 0)
    body(nq - 1)
    step(None, (nq - 1, nk - 1) + bufs[last])
    finish(nq - 1)


def _key_positions(T):
    t = jnp.arange(T, dtype=jnp.int32)
    hi = ((t // _POS_SPLIT) * _POS_SPLIT).astype(F32)
    lo = (t % _POS_SPLIT).astype(F32)
    one = jnp.ones((T,), F32)
    cols = jnp.stack([hi, hi, hi, lo, lo, lo, one, one, one], axis=1)
    return jnp.pad(cols, ((0, 0), (0, 2 * DA_HEAD_DIM - cols.shape[1]))).astype(BF16)


def _diff_attn(slopes, qT, k, vT, lq1, lk1, lq2, lk2, g, lambda_init):
    B, _, T = qT.shape
    two_dk = 2 * DA_HEAD_DIM
    assert ATT_Q % ATT_K == 0 and T % ATT_Q == 0 and T // ATT_Q >= 2
    assert (T // ATT_K - ATT_Q // ATT_K) % 2 == 0
    vec = lambda: pl.BlockSpec((1, DA_HEAD_DIM), lambda b, h: (0, 0))
    return pl.pallas_call(
        functools.partial(_diff_attn_kernel, lambda_init=lambda_init),
        grid=(B, DA_HEADS),
        in_specs=[
            pl.BlockSpec(memory_space=pltpu.SMEM),
            pl.BlockSpec((None, two_dk, T), lambda b, h: (b, h, 0)),
            pl.BlockSpec((None, T, two_dk), lambda b, h: (b, 0, h)),
            pl.BlockSpec((T, two_dk), lambda b, h: (0, 0)),
            pl.BlockSpec((None, DA_V_DIM, T), lambda b, h: (b, h, 0)),
            vec(), vec(), vec(), vec(),
            pl.BlockSpec((1, DA_V_DIM), lambda b, h: (0, 0)),
        ],
        out_specs=pl.BlockSpec((None, T, DA_V_DIM), lambda b, h: (b, 0, h)),
        out_shape=jax.ShapeDtypeStruct((B, T, DA_WIDTH), BF16),
        scratch_shapes=[
            pltpu.VMEM((2 * two_dk, 2 * ATT_Q), BF16),
            pltpu.VMEM((_BIAS_ROWS, 2 * ATT_Q), F32),
            pltpu.VMEM((ATT_K, 2 * ATT_Q), F32),
            pltpu.VMEM((ATT_K, 2 * ATT_Q), F32),
            pltpu.VMEM((1, 2 * ATT_Q), F32),
            pltpu.VMEM((1, 2 * ATT_Q), F32),
            pltpu.VMEM((1, 2 * ATT_Q), F32),
            pltpu.VMEM((_ACC_ROWS, 2 * ATT_Q), F32),
            pltpu.VMEM((ATT_K, ATT_K), F32),
        ],
        compiler_params=_params(("arbitrary", "arbitrary")),
        name="diff_attn",
    )(slopes, qT, k, _key_positions(T), vT, lq1, lk1, lq2, lk2, g)


_FFN_SPLIT = 2
_FFN_UNROLL = 8


def _proj_ffn_kernel(h0p_ref, h0_ref, h0n_ref, ygp_ref, yg_ref, ygn_ref, ydp_ref, yd_ref, ydn_ref,
                     ymp_ref, ym_ref, ymn_ref, wout_ref, g1_ref, b1_ref,
                     wup_ref, cw_ref, cb_ref, wdn_ref, g_ref, b_ref, o_ref,
                     hext_ref, h1_ref, a0_ref, a1_ref, f_ref, *, steps_per_seq):
    rows = h0_ref.shape[0]
    n_chunks = wdn_ref.shape[0] // FFN_COLS
    i = pl.program_id(0) % steps_per_seq

    def with_halo(prev_ref, main_ref, next_ref):
        return jnp.concatenate([prev_ref[...], main_ref[...], next_ref[...]], axis=0)

    y = (_dot(with_halo(ygp_ref, yg_ref, ygn_ref), wout_ref[:G_WIDTH])
         + _dot(with_halo(ydp_ref, yd_ref, ydn_ref), wout_ref[G_WIDTH:G_WIDTH + DA_WIDTH])
         + _dot(with_halo(ymp_ref, ym_ref, ymn_ref), wout_ref[G_WIDTH + DA_WIDTH:]))
    h1e = _layer_norm(ALPHA * with_halo(h0p_ref, h0_ref, h0n_ref) + y, g1_ref[...], b1_ref[...])
    h1_ref[...] = h1e[HALO:HALO + rows]
    prev = h1e[:HALO].astype(BF16)
    nxt = h1e[HALO + rows:].astype(BF16)
    hext_ref[:HALO] = jnp.where(i == 0, jnp.zeros_like(prev), prev)
    hext_ref[HALO:HALO + rows] = h1e[HALO:HALO + rows].astype(BF16)
    hext_ref[HALO + rows:] = jnp.where(i == steps_per_seq - 1, jnp.zeros_like(nxt), nxt)

    def columns(c):
        gate = pl.ds(pl.multiple_of(c * FFN_COLS, FFN_COLS), FFN_COLS)
        val = pl.ds(pl.multiple_of(D_FF + c * FFN_COLS, FFN_COLS), FFN_COLS)
        return gate, val

    def stage(up, down, first=False):
        ext_rows = rows + 2 * HALO
        up_rows = ext_rows // _FFN_SPLIT
        down_rows = rows // (2 * _FFN_SPLIT)
        if up is not None:
            up_cols = columns(up[0])

            def project(piece):
                half, part = divmod(piece, _FFN_SPLIT)
                r0 = part * up_rows
                up[1][half, r0:r0 + up_rows, :] = _dot(hext_ref[r0:r0 + up_rows, :],
                                                       wup_ref[:, up_cols[half]])

        if down is not None:
            c, a_ref = down
            down_cols = columns(c)
            w = wdn_ref[pl.ds(pl.multiple_of(c * FFN_COLS, FFN_COLS), FFN_COLS), :]

            def conv(half, r0):
                cols = down_cols[half]
                cw = cw_ref[:, cols]
                lo = HALO + r0
                return (cw[0:1] * a_ref[half, lo - 1:lo - 1 + down_rows, :]
                        + cw[1:2] * a_ref[half, lo:lo + down_rows, :]
                        + cw[2:3] * a_ref[half, lo + 1:lo + 1 + down_rows, :]
                        + cb_ref[:, cols])

            def contract(piece):
                r0 = piece * down_rows
                act = (_gelu(conv(0, r0)) * conv(1, r0)).astype(BF16)
                if first:
                    f_ref[r0:r0 + down_rows, :] = _dot(act, w)
                else:
                    f_ref[r0:r0 + down_rows, :] += _dot(act, w)

        for piece in range(2 * _FFN_SPLIT):
            if up is not None:
                project(piece)
            if down is not None:
                contract(piece)

    stage((0, a0_ref), None)
    stage((1, a1_ref), (0, a0_ref), first=True)

    def group(p, carry):
        c = _FFN_UNROLL * p + 1
        for u in range(0, _FFN_UNROLL, 2):
            stage((c + u + 1, a0_ref), (c + u, a1_ref))
            stage((c + u + 2, a1_ref), (c + u + 1, a0_ref))
        return carry

    lax.fori_loop(0, (n_chunks - 3) // _FFN_UNROLL, group, 0)
    stage((n_chunks - 1, a0_ref), (n_chunks - 2, a1_ref))
    stage(None, (n_chunks - 1, a0_ref))
    o_ref[...] = _layer_norm(ALPHA * h1_ref[...] + f_ref[...], g_ref[...], b_ref[...])


def _proj_ffn(h0, yg, yd, ym, w_out, g1, b1, w_up, conv_w, conv_b, w_down, g2, b2, seq_len):
    N, D = h0.shape
    R = FFN_ROWS
    halo_per_step = R // HALO
    n_halo = N // HALO
    n_chunks = D_FF // FFN_COLS
    assert n_chunks >= 3 and (n_chunks - 3) % _FFN_UNROLL == 0
    const = lambda shape: pl.BlockSpec(shape, lambda i: (0, 0))

    def with_halo(width):
        return [
            pl.BlockSpec((HALO, width), lambda i: (jnp.maximum(i * halo_per_step - 1, 0), 0)),
            pl.BlockSpec((R, width), lambda i: (i, 0)),
            pl.BlockSpec((HALO, width),
                         lambda i: (jnp.minimum((i + 1) * halo_per_step, n_halo - 1), 0)),
        ]

    return pl.pallas_call(
        functools.partial(_proj_ffn_kernel, steps_per_seq=seq_len // R),
        grid=(N // R,),
        in_specs=(with_halo(D) + with_halo(G_WIDTH) + with_halo(DA_WIDTH) + with_halo(M_WIDTH)
                  + [const((D, D)), const((1, D)), const((1, D)),
                     const((D, 2 * D_FF)), const((CONV_W, 2 * D_FF)), const((1, 2 * D_FF)),
                     const((D_FF, D)), const((1, D)), const((1, D))]),
        out_specs=pl.BlockSpec((R, D), lambda i: (i, 0)),
        out_shape=jax.ShapeDtypeStruct((N, D), F32),
        scratch_shapes=[
            pltpu.VMEM((R + 2 * HALO, D), BF16),
            pltpu.VMEM((R, D), F32),
            pltpu.VMEM((2, R + 2 * HALO, FFN_COLS), F32),
            pltpu.VMEM((2, R + 2 * HALO, FFN_COLS), F32),
            pltpu.VMEM((R, D), F32),
        ],
        compiler_params=_params(("arbitrary",)),
        name="proj_ffn",
    )(h0, h0, h0, yg, yg, yg, yd, yd, yd, ym, ym, ym, w_out, g1, b1,
      w_up, conv_w, conv_b, w_down, g2, b2)


def _alibi_slopes(n):
    return jnp.asarray([2.0 ** (-8.0 * (i + 1) / n) for i in range(n)], F32)


def kernel(x, mem, ln_emb_g, ln_emb_b, w_in, gmlp_ln_g, gmlp_ln_b, gmlp_ws, gmlp_bs, lambda_q1, lambda_k1, lambda_q2, lambda_k2, da_subln_g, mem_ln_g, mem_ln_b, w_mem_kv, w_out, ln1_g, ln1_b, w_up, conv_w, conv_b, w_down, ln2_g, ln2_b):
    B, T, D = x.shape
    assert DEPTH == 1 and w_in.shape[0] == 1
    assert T % PROJ_ROWS == 0 and T % ATT_Q == 0 and T % ATT_K == 0 and T % FFN_ROWS == 0
    assert D_FF % FFN_COLS == 0
    l = 0
    lambda_init = 0.8 - 0.6 * math.exp(-0.3 * l)
    row = lambda a: a.reshape(1, -1)

    w = w_in[l]
    u_end, v_end = G_WIDTH, 2 * G_WIDTH
    q_end = v_end + DA_QK
    k_end = q_end + DA_QK
    vd_end = k_end + DA_WIDTH
    w_a = jnp.concatenate([w[:, :v_end], w[:, q_end:k_end], w[:, vd_end:]], axis=1).astype(BF16)
    w_bT = jnp.concatenate([w[:, v_end:q_end], w[:, k_end:vd_end]], axis=1).T.astype(BF16)
    w_kmT = w_mem_kv[l][:, :M_WIDTH].T.astype(BF16)
    w_vm = w_mem_kv[l][:, M_WIDTH:].astype(BF16)
    ws = gmlp_ws[l].reshape(G_HEADS * CHUNK, CHUNK).astype(BF16)
    bs_tile = jnp.repeat(gmlp_bs[l].T, G_DIM, axis=1)

    kmT, vm = _mem_kv(mem, row(mem_ln_g[l]), row(mem_ln_b[l]), w_kmT, w_vm)
    h0, yg, ym, qT, k, vT = _in_proj(
        x, row(ln_emb_g), row(ln_emb_b), w_a, w_bT, row(gmlp_ln_g[l]), row(gmlp_ln_b[l]),
        ws, bs_tile, kmT, vm)
    yd = _diff_attn(_alibi_slopes(DA_HEADS), qT, k, vT, row(lambda_q1[l]), row(lambda_k1[l]),
                    row(lambda_q2[l]), row(lambda_k2[l]), row(da_subln_g[l]), lambda_init)

    N = B * T
    out = _proj_ffn(h0.reshape(N, D), yg.reshape(N, G_WIDTH), yd.reshape(N, DA_WIDTH),
                    ym.reshape(N, M_WIDTH), w_out[l].astype(BF16), row(ln1_g[l]), row(ln1_b[l]),
                    w_up[l].astype(BF16), conv_w[l], row(conv_b[l]), w_down[l].astype(BF16),
                    row(ln2_g[l]), row(ln2_b[l]), T)
    return out.reshape(B, T, D)
```

```python
import functools
import math

import jax
import jax.numpy as jnp
from jax import lax
from jax.experimental import pallas as pl
from jax.experimental.pallas import tpu as pltpu

F32 = jnp.float32
BF16 = jnp.bfloat16

D_MODEL = 1024
DEPTH = 1
MEM_LEN = 256
CHUNK = 128
G_HEADS = 4
G_WIDTH = D_MODEL // 4
G_DIM = G_WIDTH // G_HEADS
DA_HEADS = 4
DA_WIDTH = D_MODEL // 2
DA_V_DIM = DA_WIDTH // DA_HEADS
DA_HEAD_DIM = DA_V_DIM // 2
DA_QK = DA_HEADS * 2 * DA_HEAD_DIM
M_HEADS = 4
M_WIDTH = D_MODEL // 4
M_HEAD_DIM = M_WIDTH // M_HEADS
D_FF = 2816
CONV_W = 3
LN_EPS = 1e-5
ALPHA = (2.0 * DEPTH) ** 0.25
LOG2E = math.log2(math.e)

V7X_VMEM_LIMIT_BYTES = 56 * 1024 * 1024
BF16_SUBLANE_TILE = 16

PROJ_ROWS = 512
ATT_Q = 1024
ATT_K = 512
FFN_ROWS = 512
FFN_COLS = 256
HALO = BF16_SUBLANE_TILE

NT_DIMS = (((1,), (1,)), ((), ()))


def _layer_norm(x, g, b):
    mu = jnp.mean(x, axis=-1, keepdims=True)
    xc = x - mu
    var = jnp.mean(xc * xc, axis=-1, keepdims=True)
    return xc * lax.rsqrt(var + LN_EPS) * g + b


def _gelu(x):
    return 0.5 * x * (1.0 + lax.erf(x * (2.0 ** -0.5)))


def _dot(a, b):
    return jnp.dot(a, b, preferred_element_type=F32)


def _params(semantics):
    return pltpu.CompilerParams(dimension_semantics=semantics,
                                vmem_limit_bytes=V7X_VMEM_LIMIT_BYTES)


def _mem_kv_kernel(mem_ref, g_ref, b_ref, wkT_ref, wv_ref, kmT_ref, vm_ref):
    m = _layer_norm(mem_ref[...], g_ref[...], b_ref[...]).astype(BF16)
    kT = lax.dot_general(wkT_ref[...], m, NT_DIMS, preferred_element_type=F32)
    kT = kT * (M_HEAD_DIM ** -0.5)
    v = _dot(m, wv_ref[...])
    head_of_row = lax.broadcasted_iota(jnp.int32, (M_WIDTH, MEM_LEN), 0) // M_HEAD_DIM
    head_of_col = lax.broadcasted_iota(jnp.int32, (MEM_LEN, M_WIDTH), 1) // M_HEAD_DIM
    for h in range(M_HEADS):
        kmT_ref[h] = jnp.where(head_of_row == h, kT, 0.0).astype(BF16)
        vm_ref[h] = jnp.where(head_of_col == h, v, 0.0).astype(BF16)


def _mem_kv(mem, g, b, wkT, wv):
    B = mem.shape[0]
    row = lambda: pl.BlockSpec((1, D_MODEL), lambda i: (0, 0))
    return pl.pallas_call(
        _mem_kv_kernel,
        grid=(B,),
        in_specs=[
            pl.BlockSpec((None, MEM_LEN, D_MODEL), lambda i: (i, 0, 0)),
            row(), row(),
            pl.BlockSpec((M_WIDTH, D_MODEL), lambda i: (0, 0)),
            pl.BlockSpec((D_MODEL, M_WIDTH), lambda i: (0, 0)),
        ],
        out_specs=[
            pl.BlockSpec((None, M_HEADS, M_WIDTH, MEM_LEN), lambda i: (i, 0, 0, 0)),
            pl.BlockSpec((None, M_HEADS, MEM_LEN, M_WIDTH), lambda i: (i, 0, 0, 0)),
        ],
        out_shape=[
            jax.ShapeDtypeStruct((B, M_HEADS, M_WIDTH, MEM_LEN), BF16),
            jax.ShapeDtypeStruct((B, M_HEADS, MEM_LEN, M_WIDTH), BF16),
        ],
        compiler_params=_params(("arbitrary",)),
        name="mem_kv",
    )(mem, g, b, wkT, wv)


_A_U = 0
_A_V = G_WIDTH
_A_K = 2 * G_WIDTH
_A_QM = 2 * G_WIDTH + DA_QK
_A_END = _A_QM + M_WIDTH


def _in_proj_kernel(x_ref, eg_ref, eb_ref, wa_ref, wbT_ref, gg_ref, gb_ref, ws_ref, bs_ref,
                    kmT_ref, vm_ref,
                    h0_ref, yg_ref, ym_ref, qT_ref, k_ref, vT_ref):
    rows = x_ref.shape[0]
    h = _layer_norm(x_ref[...], eg_ref[...], eb_ref[...])
    h0_ref[...] = h
    hb = h.astype(BF16)

    pa = _dot(hb, wa_ref[...])
    pbT = lax.dot_general(wbT_ref[...], hb, NT_DIMS, preferred_element_type=F32)
    qT_ref[...] = (pbT[:DA_QK] * (DA_HEAD_DIM ** -0.5 * LOG2E)).astype(BF16)
    vT_ref[...] = pbT[DA_QK:].astype(BF16)
    k_ref[...] = pa[:, _A_K:_A_QM].astype(BF16)

    u = _gelu(pa[:, _A_U:_A_V])
    v = _layer_norm(_gelu(pa[:, _A_V:_A_K]), gg_ref[...], gb_ref[...]).astype(BF16)
    group_of_lane = lax.broadcasted_iota(jnp.int32, (CHUNK, G_WIDTH), 1) // G_DIM
    ws = ws_ref[...]
    bs = bs_ref[...]
    for c in range(rows // CHUNK):
        sl = slice(c * CHUNK, (c + 1) * CHUNK)
        r = _dot(ws, v[sl])
        s = r[(G_HEADS - 1) * CHUNK:]
        for g in range(G_HEADS - 2, -1, -1):
            s = jnp.where(group_of_lane == g, r[g * CHUNK:(g + 1) * CHUNK], s)
        yg_ref[sl, :] = (u[sl] * (s + bs)).astype(BF16)

    qm = pa[:, _A_QM:_A_END].astype(BF16)
    o = jnp.zeros((rows, M_WIDTH), F32)
    for hd in range(M_HEADS):
        s = _dot(qm, kmT_ref[hd])
        p = jnp.exp(s - jnp.max(s, axis=-1, keepdims=True))
        p = p / jnp.sum(p, axis=-1, keepdims=True)
        o = o + _dot(p.astype(BF16), vm_ref[hd])
    ym_ref[...] = o.astype(BF16)


def _in_proj(x, eg, eb, w_a, w_bT, gg, gb, ws, bs_tile, kmT, vm):
    B, T, D = x.shape
    R = PROJ_ROWS
    const2 = lambda shape: pl.BlockSpec(shape, lambda b, i: (0, 0))
    return pl.pallas_call(
        _in_proj_kernel,
        grid=(B, T // R),
        in_specs=[
            pl.BlockSpec((None, R, D), lambda b, i: (b, i, 0)),
            const2((1, D)), const2((1, D)),
            const2((D, _A_END)),
            const2((DA_QK + DA_WIDTH, D)),
            const2((1, G_WIDTH)), const2((1, G_WIDTH)),
            const2((G_HEADS * CHUNK, CHUNK)),
            const2((CHUNK, G_WIDTH)),
            pl.BlockSpec((None, M_HEADS, M_WIDTH, MEM_LEN), lambda b, i: (b, 0, 0, 0)),
            pl.BlockSpec((None, M_HEADS, MEM_LEN, M_WIDTH), lambda b, i: (b, 0, 0, 0)),
        ],
        out_specs=[
            pl.BlockSpec((None, R, D), lambda b, i: (b, i, 0)),
            pl.BlockSpec((None, R, G_WIDTH), lambda b, i: (b, i, 0)),
            pl.BlockSpec((None, R, M_WIDTH), lambda b, i: (b, i, 0)),
            pl.BlockSpec((None, DA_QK, R), lambda b, i: (b, 0, i)),
            pl.BlockSpec((None, R, DA_QK), lambda b, i: (b, i, 0)),
            pl.BlockSpec((None, DA_WIDTH, R), lambda b, i: (b, 0, i)),
        ],
        out_shape=[
            jax.ShapeDtypeStruct((B, T, D), F32),
            jax.ShapeDtypeStruct((B, T, G_WIDTH), BF16),
            jax.ShapeDtypeStruct((B, T, M_WIDTH), BF16),
            jax.ShapeDtypeStruct((B, DA_QK, T), BF16),
            jax.ShapeDtypeStruct((B, T, DA_QK), BF16),
            jax.ShapeDtypeStruct((B, DA_WIDTH, T), BF16),
        ],
        compiler_params=_params(("arbitrary", "arbitrary")),
        name="in_proj",
    )(x, eg, eb, w_a, w_bT, gg, gb, ws, bs_tile, kmT, vm)


_POS_SPLIT = 64
_BIAS_ROWS = BF16_SUBLANE_TILE
_ACC_ROWS = DA_V_DIM + BF16_SUBLANE_TILE


def _bf16_split3(x):
    a = x.astype(BF16).astype(F32)
    b = (x - a).astype(BF16).astype(F32)
    c = x - a - b
    return a, b, c


def _diff_attn_kernel(slopes_ref, qT_ref, k_ref, pos_ref, vT_ref, lq1_ref, lk1_ref, lq2_ref,
                      lk2_ref, g_ref, o_ref, qa_ref, base_ref, sa_ref, sb_ref, bma_ref, bmb_ref,
                      m_ref, acc_ref, diag_ref, kaug_ref, vaug_ref, *, lambda_init):
    tq = ATT_Q
    tk = ATT_K
    two_dk = 2 * DA_HEAD_DIM
    nk = k_ref.shape[0] // tk
    nq = qT_ref.shape[1] // tq
    sub = tq // tk
    c = slopes_ref[pl.program_id(1)] * LOG2E

    kk = lax.broadcasted_iota(jnp.int32, (tk, tk), 0)
    qq = lax.broadcasted_iota(jnp.int32, (tk, tk), 1)
    diag_ref[...] = -c * jnp.abs(kk - qq).astype(F32)

    lane = lax.broadcasted_iota(jnp.int32, (1, 2 * tq), 1)
    col = jnp.where(lane >= tq, lane - tq, lane)
    group = col // tk

    lam = (jnp.exp(jnp.sum(lq1_ref[...] * lk1_ref[...], axis=-1, keepdims=True))
           - jnp.exp(jnp.sum(lq2_ref[...] * lk2_ref[...], axis=-1, keepdims=True))
           + lambda_init)
    qa_ref[two_dk + _BIAS_ROWS:, :] = jnp.zeros((two_dk - _BIAS_ROWS, 2 * tq), BF16)

    T = k_ref.shape[0]
    kaug_ref[:, :two_dk] = k_ref[...]
    kaug_ref[:, two_dk:] = pos_ref[...]
    vaug_ref[:DA_V_DIM, :] = vT_ref[...]
    vaug_ref[DA_V_DIM:, :] = jnp.where(
        lax.broadcasted_iota(jnp.int32, (BF16_SUBLANE_TILE, T), 0) == 0, 1.0, 0.0).astype(BF16)

    def load_queries(qn):
        qt = qT_ref[:, pl.ds(pl.multiple_of(qn * tq, tq), tq)]
        zero = jnp.zeros((DA_HEAD_DIM, tq), BF16)
        qa_ref[:DA_HEAD_DIM, :] = jnp.concatenate([qt[:DA_HEAD_DIM], zero], axis=1)
        qa_ref[DA_HEAD_DIM:two_dk, :] = jnp.concatenate([zero, qt[DA_HEAD_DIM:]], axis=1)
        qpos = (qn * tq + col).astype(F32)
        c1, c2, c3 = _bf16_split3(jnp.full((1, 2 * tq), c, F32))
        v1, v2, v3 = _bf16_split3(c * qpos)
        r = lax.broadcasted_iota(jnp.int32, (_BIAS_ROWS, 2 * tq), 0)
        base = jnp.zeros((_BIAS_ROWS, 2 * tq), F32)
        for idx, val in enumerate((-c1, -c2, -c3, -c1, -c2, -c3, v1, v2, v3)):
            base = jnp.where(r == idx, val, base)
        base_ref[...] = base

    def block_start(qn, t):
        return pl.multiple_of(lax.rem(qn * sub + t, nk) * tk, tk)

    n_strips = 2 * sub

    def step(score_args, consume_args):
        if score_args is not None:
            qn, t, s_ref, bm_ref, straddles = score_args
            start = block_start(qn, t)
            if straddles:
                sign = jnp.where(group < t, 1.0, jnp.where(group > t, -1.0, 0.0)).astype(F32)
            else:
                sign = jnp.where(start > qn * tq, 1.0, -1.0).astype(F32)
            qa_ref[two_dk:two_dk + _BIAS_ROWS, :] = (sign * base_ref[...]).astype(BF16)
            kb = kaug_ref[pl.ds(start, tk), :]
        if consume_args is not None:
            cqn, ct, cs_ref, cbm_ref = consume_args
            vb = vaug_ref[:, pl.ds(block_start(cqn, ct), tk)]
        for j in range(n_strips):
            cols = slice(j * tk, (j + 1) * tk)
            if score_args is not None:
                s = _dot(kb, qa_ref[:, cols])
                if straddles and j % sub == t:
                    s = s + diag_ref[...]
                s_ref[:, cols] = s
                bm_ref[:, cols] = jnp.max(s, axis=0, keepdims=True)
            if consume_args is not None:
                m_old = m_ref[:, cols]
                m_new = jnp.maximum(m_old, cbm_ref[:, cols])
                alpha = jnp.exp2(m_old - m_new)
                m_ref[:, cols] = m_new
                p = jnp.exp2(cs_ref[:, cols] - m_new).astype(BF16)
                acc_ref[:, cols] = alpha * acc_ref[:, cols] + _dot(vb, p)

    def reset_stats():
        acc_ref[...] = jnp.zeros_like(acc_ref)
        m_ref[...] = jnp.full(m_ref.shape, -1e30, F32)

    def finish(qn):
        inv_l = 1.0 / acc_ref[DA_V_DIM:DA_V_DIM + 1, :]
        oT = (acc_ref[:DA_V_DIM, :tq] * inv_l[:, :tq]
              - lam * (acc_ref[:DA_V_DIM, tq:] * inv_l[:, tq:]))
        o = oT.T
        ms = jnp.mean(o * o, axis=-1, keepdims=True)
        o = o * lax.rsqrt(ms + LN_EPS) * g_ref[...] * (1.0 - lambda_init)
        o_ref[pl.ds(pl.multiple_of(qn * tq, tq), tq), :] = o.astype(o_ref.dtype)

    bufs = ((sa_ref, bma_ref), (sb_ref, bmb_ref))

    last = (nk - 1) % 2

    def head_rest(qn):
        for t in range(1, sub):
            step((qn, t) + bufs[t % 2] + (True,), (qn, t - 1) + bufs[(t - 1) % 2])

    def body(qn):
        def pair(i, inner):
            t = sub + 2 * i
            step((qn, t) + bufs[sub % 2] + (False,), (qn, t - 1) + bufs[(sub - 1) % 2])
            step((qn, t + 1) + bufs[(sub + 1) % 2] + (False,), (qn, t) + bufs[sub % 2])
            return inner

        for i in range((nk - sub) // 2):
            pair(i, 0)

    load_queries(0)
    step((0, 0) + bufs[0] + (True,), None)
    reset_stats()
    head_rest(0)

    def query_block(qn, carry):
        body(qn)
        load_queries(qn + 1)
        step((qn + 1, 0) + bufs[0] + (True,), (qn, nk - 1) + bufs[last])
        finish(qn)
        reset_stats()
        head_rest(qn + 1)
        return carry

    lax.fori_loop(0, nq - 1, query_block, 0)
    body(nq - 1)
    step(None, (nq - 1, nk - 1) + bufs[last])
    finish(nq - 1)


def _key_positions(T):
    t = jnp.arange(T, dtype=jnp.int32)
    hi = ((t // _POS_SPLIT) * _POS_SPLIT).astype(F32)
    lo = (t % _POS_SPLIT).astype(F32)
    one = jnp.ones((T,), F32)
    cols = jnp.stack([hi, hi, hi, lo, lo, lo, one, one, one], axis=1)
    return jnp.pad(cols, ((0, 0), (0, 2 * DA_HEAD_DIM - cols.shape[1]))).astype(BF16)


def _diff_attn(slopes, qT, k, vT, lq1, lk1, lq2, lk2, g, lambda_init):
    B, _, T = qT.shape
    two_dk = 2 * DA_HEAD_DIM
    assert ATT_Q % ATT_K == 0 and T % ATT_Q == 0 and T // ATT_Q >= 2
    assert (T // ATT_K - ATT_Q // ATT_K) % 2 == 0
    vec = lambda: pl.BlockSpec((1, DA_HEAD_DIM), lambda b, h: (0, 0))
    return pl.pallas_call(
        functools.partial(_diff_attn_kernel, lambda_init=lambda_init),
        grid=(B, DA_HEADS),
        in_specs=[
            pl.BlockSpec(memory_space=pltpu.SMEM),
            pl.BlockSpec((None, two_dk, T), lambda b, h: (b, h, 0)),
            pl.BlockSpec((None, T, two_dk), lambda b, h: (b, 0, h)),
            pl.BlockSpec((T, two_dk), lambda b, h: (0, 0)),
            pl.BlockSpec((None, DA_V_DIM, T), lambda b, h: (b, h, 0)),
            vec(), vec(), vec(), vec(),
            pl.BlockSpec((1, DA_V_DIM), lambda b, h: (0, 0)),
        ],
        out_specs=pl.BlockSpec((None, T, DA_V_DIM), lambda b, h: (b, 0, h)),
        out_shape=jax.ShapeDtypeStruct((B, T, DA_WIDTH), BF16),
        scratch_shapes=[
            pltpu.VMEM((2 * two_dk, 2 * ATT_Q), BF16),
            pltpu.VMEM((_BIAS_ROWS, 2 * ATT_Q), F32),
            pltpu.VMEM((ATT_K, 2 * ATT_Q), F32),
            pltpu.VMEM((ATT_K, 2 * ATT_Q), F32),
            pltpu.VMEM((1, 2 * ATT_Q), F32),
            pltpu.VMEM((1, 2 * ATT_Q), F32),
            pltpu.VMEM((1, 2 * ATT_Q), F32),
            pltpu.VMEM((_ACC_ROWS, 2 * ATT_Q), F32),
            pltpu.VMEM((ATT_K, ATT_K), F32),
            pltpu.VMEM((T, 2 * two_dk), BF16),
            pltpu.VMEM((_ACC_ROWS, T), BF16),
        ],
        compiler_params=_params(("arbitrary", "arbitrary")),
        name="diff_attn",
    )(slopes, qT, k, _key_positions(T), vT, lq1, lk1, lq2, lk2, g)


_FFN_SPLIT = 2
_FFN_UNROLL = 8


def _proj_ffn_kernel(h0p_ref, h0_ref, h0n_ref, ygp_ref, yg_ref, ygn_ref, ydp_ref, yd_ref, ydn_ref,
                     ymp_ref, ym_ref, ymn_ref, wout_ref, g1_ref, b1_ref,
                     wup_ref, cw_ref, cb_ref, wdn_ref, g_ref, b_ref, o_ref,
                     hext_ref, h1_ref, a0_ref, a1_ref, f_ref, *, steps_per_seq):
    rows = h0_ref.shape[0]
    n_chunks = wdn_ref.shape[0] // FFN_COLS
    i = pl.program_id(0) % steps_per_seq

    def with_halo(prev_ref, main_ref, next_ref):
        return jnp.concatenate([prev_ref[...], main_ref[...], next_ref[...]], axis=0)

    y = (_dot(with_halo(ygp_ref, yg_ref, ygn_ref), wout_ref[:G_WIDTH])
         + _dot(with_halo(ydp_ref, yd_ref, ydn_ref), wout_ref[G_WIDTH:G_WIDTH + DA_WIDTH])
         + _dot(with_halo(ymp_ref, ym_ref, ymn_ref), wout_ref[G_WIDTH + DA_WIDTH:]))
    h1e = _layer_norm(ALPHA * with_halo(h0p_ref, h0_ref, h0n_ref) + y, g1_ref[...], b1_ref[...])
    h1_ref[...] = h1e[HALO:HALO + rows]
    prev = h1e[:HALO].astype(BF16)
    nxt = h1e[HALO + rows:].astype(BF16)
    hext_ref[:HALO] = jnp.where(i == 0, jnp.zeros_like(prev), prev)
    hext_ref[HALO:HALO + rows] = h1e[HALO:HALO + rows].astype(BF16)
    hext_ref[HALO + rows:] = jnp.where(i == steps_per_seq - 1, jnp.zeros_like(nxt), nxt)

    def columns(c):
        gate = pl.ds(pl.multiple_of(c * FFN_COLS, FFN_COLS), FFN_COLS)
        val = pl.ds(pl.multiple_of(D_FF + c * FFN_COLS, FFN_COLS), FFN_COLS)
        return gate, val

    def stage(up, down, first=False):
        ext_rows = rows + 2 * HALO
        up_rows = ext_rows // _FFN_SPLIT
        down_rows = rows // (2 * _FFN_SPLIT)
        if up is not None:
            up_cols = columns(up[0])

            def project(piece):
                half, part = divmod(piece, _FFN_SPLIT)
                r0 = part * up_rows
                up[1][half, r0:r0 + up_rows, :] = _dot(hext_ref[r0:r0 + up_rows, :],
                                                       wup_ref[:, up_cols[half]])

        if down is not None:
            c, a_ref = down
            down_cols = columns(c)
            w = wdn_ref[pl.ds(pl.multiple_of(c * FFN_COLS, FFN_COLS), FFN_COLS), :]

            def conv(half, r0):
                cols = down_cols[half]
                cw = cw_ref[:, cols]
                lo = HALO + r0
                return (cw[0:1] * a_ref[half, lo - 1:lo - 1 + down_rows, :]
                        + cw[1:2] * a_ref[half, lo:lo + down_rows, :]
                        + cw[2:3] * a_ref[half, lo + 1:lo + 1 + down_rows, :]
                        + cb_ref[:, cols])

            def contract(piece):
                r0 = piece * down_rows
                act = (_gelu(conv(0, r0)) * conv(1, r0)).astype(BF16)
                if first:
                    f_ref[r0:r0 + down_rows, :] = _dot(act, w)
                else:
                    f_ref[r0:r0 + down_rows, :] += _dot(act, w)

        for piece in range(2 * _FFN_SPLIT):
            if up is not None:
                project(piece)
            if down is not None:
                contract(piece)

    stage((0, a0_ref), None)
    stage((1, a1_ref), (0, a0_ref), first=True)

    def group(p, carry):
        c = _FFN_UNROLL * p + 1
        for u in range(0, _FFN_UNROLL, 2):
            stage((c + u + 1, a0_ref), (c + u, a1_ref))
            stage((c + u + 2, a1_ref), (c + u + 1, a0_ref))
        return carry

    lax.fori_loop(0, (n_chunks - 3) // _FFN_UNROLL, group, 0)
    stage((n_chunks - 1, a0_ref), (n_chunks - 2, a1_ref))
    stage(None, (n_chunks - 1, a0_ref))
    o_ref[...] = _layer_norm(ALPHA * h1_ref[...] + f_ref[...], g_ref[...], b_ref[...])


def _proj_ffn(h0, yg, yd, ym, w_out, g1, b1, w_up, conv_w, conv_b, w_down, g2, b2, seq_len):
    N, D = h0.shape
    R = FFN_ROWS
    halo_per_step = R // HALO
    n_halo = N // HALO
    n_chunks = D_FF // FFN_COLS
    assert n_chunks >= 3 and (n_chunks - 3) % _FFN_UNROLL == 0
    const = lambda shape: pl.BlockSpec(shape, lambda i: (0, 0))

    def with_halo(width):
        return [
            pl.BlockSpec((HALO, width), lambda i: (jnp.maximum(i * halo_per_step - 1, 0), 0)),
            pl.BlockSpec((R, width), lambda i: (i, 0)),
            pl.BlockSpec((HALO, width),
                         lambda i: (jnp.minimum((i + 1) * halo_per_step, n_halo - 1), 0)),
        ]

    return pl.pallas_call(
        functools.partial(_proj_ffn_kernel, steps_per_seq=seq_len // R),
        grid=(N // R,),
        in_specs=(with_halo(D) + with_halo(G_WIDTH) + with_halo(DA_WIDTH) + with_halo(M_WIDTH)
                  + [const((D, D)), const((1, D)), const((1, D)),
                     const((D, 2 * D_FF)), const((CONV_W, 2 * D_FF)), const((1, 2 * D_FF)),
                     const((D_FF, D)), const((1, D)), const((1, D))]),
        out_specs=pl.BlockSpec((R, D), lambda i: (i, 0)),
        out_shape=jax.ShapeDtypeStruct((N, D), F32),
        scratch_shapes=[
            pltpu.VMEM((R + 2 * HALO, D), BF16),
            pltpu.VMEM((R, D), F32),
            pltpu.VMEM((2, R + 2 * HALO, FFN_COLS), F32),
            pltpu.VMEM((2, R + 2 * HALO, FFN_COLS), F32),
            pltpu.VMEM((R, D), F32),
        ],
        compiler_params=_params(("arbitrary",)),
        name="proj_ffn",
    )(h0, h0, h0, yg, yg, yg, yd, yd, yd, ym, ym, ym, w_out, g1, b1,
      w_up, conv_w, conv_b, w_down, g2, b2)


def _alibi_slopes(n):
    return jnp.asarray([2.0 ** (-8.0 * (i + 1) / n) for i in range(n)], F32)


def kernel(x, mem, ln_emb_g, ln_emb_b, w_in, gmlp_ln_g, gmlp_ln_b, gmlp_ws, gmlp_bs, lambda_q1, lambda_k1, lambda_q2, lambda_k2, da_subln_g, mem_ln_g, mem_ln_b, w_mem_kv, w_out, ln1_g, ln1_b, w_up, conv_w, conv_b, w_down, ln2_g, ln2_b):
    B, T, D = x.shape
    assert DEPTH == 1 and w_in.shape[0] == 1
    assert T % PROJ_ROWS == 0 and T % ATT_Q == 0 and T % ATT_K == 0 and T % FFN_ROWS == 0
    assert D_FF % FFN_COLS == 0
    l = 0
    lambda_init = 0.8 - 0.6 * math.exp(-0.3 * l)
    row = lambda a: a.reshape(1, -1)

    w = w_in[l]
    u_end, v_end = G_WIDTH, 2 * G_WIDTH
    q_end = v_end + DA_QK
    k_end = q_end + DA_QK
    vd_end = k_end + DA_WIDTH
    w_a = jnp.concatenate([w[:, :v_end], w[:, q_end:k_end], w[:, vd_end:]], axis=1).astype(BF16)
    w_bT = jnp.concatenate([w[:, v_end:q_end], w[:, k_end:vd_end]], axis=1).T.astype(BF16)
    w_kmT = w_mem_kv[l][:, :M_WIDTH].T.astype(BF16)
    w_vm = w_mem_kv[l][:, M_WIDTH:].astype(BF16)
    ws = gmlp_ws[l].reshape(G_HEADS * CHUNK, CHUNK).astype(BF16)
    bs_tile = jnp.repeat(gmlp_bs[l].T, G_DIM, axis=1)

    kmT, vm = _mem_kv(mem, row(mem_ln_g[l]), row(mem_ln_b[l]), w_kmT, w_vm)
    h0, yg, ym, qT, k, vT = _in_proj(
        x, row(ln_emb_g), row(ln_emb_b), w_a, w_bT, row(gmlp_ln_g[l]), row(gmlp_ln_b[l]),
        ws, bs_tile, kmT, vm)
    yd = _diff_attn(_alibi_slopes(DA_HEADS), qT, k, vT, row(lambda_q1[l]), row(lambda_k1[l]),
                    row(lambda_q2[l]), row(lambda_k2[l]), row(da_subln_g[l]), lambda_init)

    N = B * T
    out = _proj_ffn(h0.reshape(N, D), yg.reshape(N, G_WIDTH), yd.reshape(N, DA_WIDTH),
                    ym.reshape(N, M_WIDTH), w_out[l].astype(BF16), row(ln1_g[l]), row(ln1_b[l]),
                    w_up[l].astype(BF16), conv_w[l], row(conv_b[l]), w_down[l].astype(BF16),
                    row(ln2_g[l]), row(ln2_b[l]), T)
    return out.reshape(B, T, D)
```

```python
import functools
import math

import jax
import jax.numpy as jnp
from jax import lax
from jax.experimental import pallas as pl
from jax.experimental.pallas import tpu as pltpu

F32 = jnp.float32
BF16 = jnp.bfloat16

D_MODEL = 1024
DEPTH = 1
MEM_LEN = 256
CHUNK = 128
G_HEADS = 4
G_WIDTH = D_MODEL // 4
G_DIM = G_WIDTH // G_HEADS
DA_HEADS = 4
DA_WIDTH = D_MODEL // 2
DA_V_DIM = DA_WIDTH // DA_HEADS
DA_HEAD_DIM = DA_V_DIM // 2
DA_QK = DA_HEADS * 2 * DA_HEAD_DIM
M_HEADS = 4
M_WIDTH = D_MODEL // 4
M_HEAD_DIM = M_WIDTH // M_HEADS
D_FF = 2816
CONV_W = 3
LN_EPS = 1e-5
ALPHA = (2.0 * DEPTH) ** 0.25
LOG2E = math.log2(math.e)

V7X_VMEM_LIMIT_BYTES = 56 * 1024 * 1024
BF16_SUBLANE_TILE = 16

PROJ_ROWS = 512
ATT_Q = 1024
ATT_K = 256
FFN_ROWS = 512
FFN_COLS = 256
HALO = BF16_SUBLANE_TILE

NT_DIMS = (((1,), (1,)), ((), ()))


def _layer_norm(x, g, b):
    mu = jnp.mean(x, axis=-1, keepdims=True)
    xc = x - mu
    var = jnp.mean(xc * xc, axis=-1, keepdims=True)
    return xc * lax.rsqrt(var + LN_EPS) * g + b


def _gelu(x):
    return 0.5 * x * (1.0 + lax.erf(x * (2.0 ** -0.5)))


def _dot(a, b):
    return jnp.dot(a, b, preferred_element_type=F32)


def _params(semantics):
    return pltpu.CompilerParams(dimension_semantics=semantics,
                                vmem_limit_bytes=V7X_VMEM_LIMIT_BYTES)


def _mem_kv_kernel(mem_ref, g_ref, b_ref, wkT_ref, wv_ref, kmT_ref, vm_ref):
    m = _layer_norm(mem_ref[...], g_ref[...], b_ref[...]).astype(BF16)
    kT = lax.dot_general(wkT_ref[...], m, NT_DIMS, preferred_element_type=F32)
    kT = kT * (M_HEAD_DIM ** -0.5)
    v = _dot(m, wv_ref[...])
    head_of_row = lax.broadcasted_iota(jnp.int32, (M_WIDTH, MEM_LEN), 0) // M_HEAD_DIM
    head_of_col = lax.broadcasted_iota(jnp.int32, (MEM_LEN, M_WIDTH), 1) // M_HEAD_DIM
    for h in range(M_HEADS):
        kmT_ref[h] = jnp.where(head_of_row == h, kT, 0.0).astype(BF16)
        vm_ref[h] = jnp.where(head_of_col == h, v, 0.0).astype(BF16)


def _mem_kv(mem, g, b, wkT, wv):
    B = mem.shape[0]
    row = lambda: pl.BlockSpec((1, D_MODEL), lambda i: (0, 0))
    return pl.pallas_call(
        _mem_kv_kernel,
        grid=(B,),
        in_specs=[
            pl.BlockSpec((None, MEM_LEN, D_MODEL), lambda i: (i, 0, 0)),
            row(), row(),
            pl.BlockSpec((M_WIDTH, D_MODEL), lambda i: (0, 0)),
            pl.BlockSpec((D_MODEL, M_WIDTH), lambda i: (0, 0)),
        ],
        out_specs=[
            pl.BlockSpec((None, M_HEADS, M_WIDTH, MEM_LEN), lambda i: (i, 0, 0, 0)),
            pl.BlockSpec((None, M_HEADS, MEM_LEN, M_WIDTH), lambda i: (i, 0, 0, 0)),
        ],
        out_shape=[
            jax.ShapeDtypeStruct((B, M_HEADS, M_WIDTH, MEM_LEN), BF16),
            jax.ShapeDtypeStruct((B, M_HEADS, MEM_LEN, M_WIDTH), BF16),
        ],
        compiler_params=_params(("arbitrary",)),
        name="mem_kv",
    )(mem, g, b, wkT, wv)


_A_U = 0
_A_V = G_WIDTH
_A_K = 2 * G_WIDTH
_A_QM = 2 * G_WIDTH + DA_QK
_A_END = _A_QM + M_WIDTH


def _in_proj_kernel(x_ref, eg_ref, eb_ref, wa_ref, wbT_ref, gg_ref, gb_ref, ws_ref, bs_ref,
                    kmT_ref, vm_ref,
                    h0_ref, yg_ref, ym_ref, qT_ref, k_ref, vT_ref):
    rows = x_ref.shape[0]
    h = _layer_norm(x_ref[...], eg_ref[...], eb_ref[...])
    h0_ref[...] = h
    hb = h.astype(BF16)

    pa = _dot(hb, wa_ref[...])
    pbT = lax.dot_general(wbT_ref[...], hb, NT_DIMS, preferred_element_type=F32)
    qT_ref[...] = (pbT[:DA_QK] * (DA_HEAD_DIM ** -0.5 * LOG2E)).astype(BF16)
    vT_ref[...] = pbT[DA_QK:].astype(BF16)
    k_ref[...] = pa[:, _A_K:_A_QM].astype(BF16)

    u = _gelu(pa[:, _A_U:_A_V])
    v = _layer_norm(_gelu(pa[:, _A_V:_A_K]), gg_ref[...], gb_ref[...]).astype(BF16)
    group_of_lane = lax.broadcasted_iota(jnp.int32, (CHUNK, G_WIDTH), 1) // G_DIM
    ws = ws_ref[...]
    bs = bs_ref[...]
    for c in range(rows // CHUNK):
        sl = slice(c * CHUNK, (c + 1) * CHUNK)
        r = _dot(ws, v[sl])
        s = r[(G_HEADS - 1) * CHUNK:]
        for g in range(G_HEADS - 2, -1, -1):
            s = jnp.where(group_of_lane == g, r[g * CHUNK:(g + 1) * CHUNK], s)
        yg_ref[sl, :] = (u[sl] * (s + bs)).astype(BF16)

    qm = pa[:, _A_QM:_A_END].astype(BF16)
    o = jnp.zeros((rows, M_WIDTH), F32)
    for hd in range(M_HEADS):
        s = _dot(qm, kmT_ref[hd])
        p = jnp.exp(s - jnp.max(s, axis=-1, keepdims=True))
        p = p / jnp.sum(p, axis=-1, keepdims=True)
        o = o + _dot(p.astype(BF16), vm_ref[hd])
    ym_ref[...] = o.astype(BF16)


def _in_proj(x, eg, eb, w_a, w_bT, gg, gb, ws, bs_tile, kmT, vm):
    B, T, D = x.shape
    R = PROJ_ROWS
    const2 = lambda shape: pl.BlockSpec(shape, lambda b, i: (0, 0))
    return pl.pallas_call(
        _in_proj_kernel,
        grid=(B, T // R),
        in_specs=[
            pl.BlockSpec((None, R, D), lambda b, i: (b, i, 0)),
            const2((1, D)), const2((1, D)),
            const2((D, _A_END)),
            const2((DA_QK + DA_WIDTH, D)),
            const2((1, G_WIDTH)), const2((1, G_WIDTH)),
            const2((G_HEADS * CHUNK, CHUNK)),
            const2((CHUNK, G_WIDTH)),
            pl.BlockSpec((None, M_HEADS, M_WIDTH, MEM_LEN), lambda b, i: (b, 0, 0, 0)),
            pl.BlockSpec((None, M_HEADS, MEM_LEN, M_WIDTH), lambda b, i: (b, 0, 0, 0)),
        ],
        out_specs=[
            pl.BlockSpec((None, R, D), lambda b, i: (b, i, 0)),
            pl.BlockSpec((None, R, G_WIDTH), lambda b, i: (b, i, 0)),
            pl.BlockSpec((None, R, M_WIDTH), lambda b, i: (b, i, 0)),
            pl.BlockSpec((None, DA_QK, R), lambda b, i: (b, 0, i)),
            pl.BlockSpec((None, R, DA_QK), lambda b, i: (b, i, 0)),
            pl.BlockSpec((None, DA_WIDTH, R), lambda b, i: (b, 0, i)),
        ],
        out_shape=[
            jax.ShapeDtypeStruct((B, T, D), F32),
            jax.ShapeDtypeStruct((B, T, G_WIDTH), BF16),
            jax.ShapeDtypeStruct((B, T, M_WIDTH), BF16),
            jax.ShapeDtypeStruct((B, DA_QK, T), BF16),
            jax.ShapeDtypeStruct((B, T, DA_QK), BF16),
            jax.ShapeDtypeStruct((B, DA_WIDTH, T), BF16),
        ],
        compiler_params=_params(("arbitrary", "arbitrary")),
        name="in_proj",
    )(x, eg, eb, w_a, w_bT, gg, gb, ws, bs_tile, kmT, vm)


_POS_SPLIT = 64
_BIAS_ROWS = BF16_SUBLANE_TILE
_ACC_ROWS = DA_V_DIM + BF16_SUBLANE_TILE


def _bf16_split3(x):
    a = x.astype(BF16).astype(F32)
    b = (x - a).astype(BF16).astype(F32)
    c = x - a - b
    return a, b, c


def _diff_attn_kernel(slopes_ref, qT_ref, k_ref, pos_ref, vT_ref, lq1_ref, lk1_ref, lq2_ref,
                      lk2_ref, g_ref, o_ref, qa_ref, base_ref, sa_ref, sb_ref, bma_ref, bmb_ref,
                      m_ref, acc_ref, diag_ref, *, lambda_init):
    tq = ATT_Q
    tk = ATT_K
    two_dk = 2 * DA_HEAD_DIM
    nk = k_ref.shape[0] // tk
    nq = qT_ref.shape[1] // tq
    sub = tq // tk
    c = slopes_ref[pl.program_id(1)] * LOG2E

    kk = lax.broadcasted_iota(jnp.int32, (tk, tk), 0)
    qq = lax.broadcasted_iota(jnp.int32, (tk, tk), 1)
    diag_ref[...] = -c * jnp.abs(kk - qq).astype(F32)

    lane = lax.broadcasted_iota(jnp.int32, (1, 2 * tq), 1)
    col = jnp.where(lane >= tq, lane - tq, lane)
    group = col // tk

    lam = (jnp.exp(jnp.sum(lq1_ref[...] * lk1_ref[...], axis=-1, keepdims=True))
           - jnp.exp(jnp.sum(lq2_ref[...] * lk2_ref[...], axis=-1, keepdims=True))
           + lambda_init)
    ones_row = jnp.where(lax.broadcasted_iota(jnp.int32, (BF16_SUBLANE_TILE, tk), 0) == 0,
                         1.0, 0.0).astype(BF16)
    qa_ref[two_dk + _BIAS_ROWS:, :] = jnp.zeros((two_dk - _BIAS_ROWS, 2 * tq), BF16)

    def load_queries(qn):
        qt = qT_ref[:, pl.ds(pl.multiple_of(qn * tq, tq), tq)]
        zero = jnp.zeros((DA_HEAD_DIM, tq), BF16)
        qa_ref[:DA_HEAD_DIM, :] = jnp.concatenate([qt[:DA_HEAD_DIM], zero], axis=1)
        qa_ref[DA_HEAD_DIM:two_dk, :] = jnp.concatenate([zero, qt[DA_HEAD_DIM:]], axis=1)
        qpos = (qn * tq + col).astype(F32)
        c1, c2, c3 = _bf16_split3(jnp.full((1, 2 * tq), c, F32))
        v1, v2, v3 = _bf16_split3(c * qpos)
        r = lax.broadcasted_iota(jnp.int32, (_BIAS_ROWS, 2 * tq), 0)
        base = jnp.zeros((_BIAS_ROWS, 2 * tq), F32)
        for idx, val in enumerate((-c1, -c2, -c3, -c1, -c2, -c3, v1, v2, v3)):
            base = jnp.where(r == idx, val, base)
        base_ref[...] = base

    def block_start(qn, t):
        return pl.multiple_of(lax.rem(qn * sub + t, nk) * tk, tk)

    n_strips = 2 * sub

    def step(score_args, consume_args):
        if score_args is not None:
            qn, t, s_ref, bm_ref, straddles = score_args
            start = block_start(qn, t)
            if straddles:
                sign = jnp.where(group < t, 1.0, jnp.where(group > t, -1.0, 0.0)).astype(F32)
            else:
                sign = jnp.where(start > qn * tq, 1.0, -1.0).astype(F32)
            qa_ref[two_dk:two_dk + _BIAS_ROWS, :] = (sign * base_ref[...]).astype(BF16)
            kb = jnp.concatenate([k_ref[pl.ds(start, tk), :], pos_ref[pl.ds(start, tk), :]],
                                 axis=1)
        if consume_args is not None:
            cqn, ct, cs_ref, cbm_ref = consume_args
            vb = jnp.concatenate([vT_ref[:, pl.ds(block_start(cqn, ct), tk)], ones_row], axis=0)
        for j in range(n_strips):
            cols = slice(j * tk, (j + 1) * tk)
            if score_args is not None:
                s = _dot(kb, qa_ref[:, cols])
                if straddles and j % sub == t:
                    s = s + diag_ref[...]
                s_ref[:, cols] = s
                bm_ref[:, cols] = jnp.max(s, axis=0, keepdims=True)
            if consume_args is not None:
                m_old = m_ref[:, cols]
                m_new = jnp.maximum(m_old, cbm_ref[:, cols])
                alpha = jnp.exp2(m_old - m_new)
                m_ref[:, cols] = m_new
                p = jnp.exp2(cs_ref[:, cols] - m_new).astype(BF16)
                acc_ref[:, cols] = alpha * acc_ref[:, cols] + _dot(vb, p)

    def reset_stats():
        acc_ref[...] = jnp.zeros_like(acc_ref)
        m_ref[...] = jnp.full(m_ref.shape, -1e30, F32)

    def finish(qn):
        inv_l = 1.0 / acc_ref[DA_V_DIM:DA_V_DIM + 1, :]
        oT = (acc_ref[:DA_V_DIM, :tq] * inv_l[:, :tq]
              - lam * (acc_ref[:DA_V_DIM, tq:] * inv_l[:, tq:]))
        o = oT.T
        ms = jnp.mean(o * o, axis=-1, keepdims=True)
        o = o * lax.rsqrt(ms + LN_EPS) * g_ref[...] * (1.0 - lambda_init)
        o_ref[pl.ds(pl.multiple_of(qn * tq, tq), tq), :] = o.astype(o_ref.dtype)

    bufs = ((sa_ref, bma_ref), (sb_ref, bmb_ref))

    last = (nk - 1) % 2

    def head_rest(qn):
        for t in range(1, sub):
            step((qn, t) + bufs[t % 2] + (True,), (qn, t - 1) + bufs[(t - 1) % 2])

    def body(qn):
        def pair(i, inner):
            t = sub + 2 * i
            step((qn, t) + bufs[sub % 2] + (False,), (qn, t - 1) + bufs[(sub - 1) % 2])
            step((qn, t + 1) + bufs[(sub + 1) % 2] + (False,), (qn, t) + bufs[sub % 2])
            return inner

        for i in range((nk - sub) // 2):
            pair(i, 0)

    load_queries(0)
    step((0, 0) + bufs[0] + (True,), None)
    reset_stats()
    head_rest(0)

    def query_block(qn, carry):
        body(qn)
        load_queries(qn + 1)
        step((qn + 1, 0) + bufs[0] + (True,), (qn, nk - 1) + bufs[last])
        finish(qn)
        reset_stats()
        head_rest(qn + 1)
        return carry

    lax.fori_loop(0, nq - 1, query_block, 0)
    body(nq - 1)
    step(None, (nq - 1, nk - 1) + bufs[last])
    finish(nq - 1)


def _key_positions(T):
    t = jnp.arange(T, dtype=jnp.int32)
    hi = ((t // _POS_SPLIT) * _POS_SPLIT).astype(F32)
    lo = (t % _POS_SPLIT).astype(F32)
    one = jnp.ones((T,), F32)
    cols = jnp.stack([hi, hi, hi, lo, lo, lo, one, one, one], axis=1)
    return jnp.pad(cols, ((0, 0), (0, 2 * DA_HEAD_DIM - cols.shape[1]))).astype(BF16)


def _diff_attn(slopes, qT, k, vT, lq1, lk1, lq2, lk2, g, lambda_init):
    B, _, T = qT.shape
    two_dk = 2 * DA_HEAD_DIM
    assert ATT_Q % ATT_K == 0 and T % ATT_Q == 0 and T // ATT_Q >= 2
    assert (T // ATT_K - ATT_Q // ATT_K) % 2 == 0
    vec = lambda: pl.BlockSpec((1, DA_HEAD_DIM), lambda b, h: (0, 0))
    return pl.pallas_call(
        functools.partial(_diff_attn_kernel, lambda_init=lambda_init),
        grid=(B, DA_HEADS),
        in_specs=[
            pl.BlockSpec(memory_space=pltpu.SMEM),
            pl.BlockSpec((None, two_dk, T), lambda b, h: (b, h, 0)),
            pl.BlockSpec((None, T, two_dk), lambda b, h: (b, 0, h)),
            pl.BlockSpec((T, two_dk), lambda b, h: (0, 0)),
            pl.BlockSpec((None, DA_V_DIM, T), lambda b, h: (b, h, 0)),
            vec(), vec(), vec(), vec(),
            pl.BlockSpec((1, DA_V_DIM), lambda b, h: (0, 0)),
        ],
        out_specs=pl.BlockSpec((None, T, DA_V_DIM), lambda b, h: (b, 0, h)),
        out_shape=jax.ShapeDtypeStruct((B, T, DA_WIDTH), BF16),
        scratch_shapes=[
            pltpu.VMEM((2 * two_dk, 2 * ATT_Q), BF16),
            pltpu.VMEM((_BIAS_ROWS, 2 * ATT_Q), F32),
            pltpu.VMEM((ATT_K, 2 * ATT_Q), F32),
            pltpu.VMEM((ATT_K, 2 * ATT_Q), F32),
            pltpu.VMEM((1, 2 * ATT_Q), F32),
            pltpu.VMEM((1, 2 * ATT_Q), F32),
            pltpu.VMEM((1, 2 * ATT_Q), F32),
            pltpu.VMEM((_ACC_ROWS, 2 * ATT_Q), F32),
            pltpu.VMEM((ATT_K, ATT_K), F32),
        ],
        compiler_params=_params(("arbitrary", "arbitrary")),
        name="diff_attn",
    )(slopes, qT, k, _key_positions(T), vT, lq1, lk1, lq2, lk2, g)


_FFN_SPLIT = 2
_FFN_UNROLL = 8


def _proj_ffn_kernel(h0p_ref, h0_ref, h0n_ref, ygp_ref, yg_ref, ygn_ref, ydp_ref, yd_ref, ydn_ref,
                     ymp_ref, ym_ref, ymn_ref, wout_ref, g1_ref, b1_ref,
                     wup_ref, cw_ref, cb_ref, wdn_ref, g_ref, b_ref, o_ref,
                     hext_ref, h1_ref, a0_ref, a1_ref, f_ref, *, steps_per_seq):
    rows = h0_ref.shape[0]
    n_chunks = wdn_ref.shape[0] // FFN_COLS
    i = pl.program_id(0) % steps_per_seq

    def with_halo(prev_ref, main_ref, next_ref):
        return jnp.concatenate([prev_ref[...], main_ref[...], next_ref[...]], axis=0)

    y = (_dot(with_halo(ygp_ref, yg_ref, ygn_ref), wout_ref[:G_WIDTH])
         + _dot(with_halo(ydp_ref, yd_ref, ydn_ref), wout_ref[G_WIDTH:G_WIDTH + DA_WIDTH])
         + _dot(with_halo(ymp_ref, ym_ref, ymn_ref), wout_ref[G_WIDTH + DA_WIDTH:]))
    h1e = _layer_norm(ALPHA * with_halo(h0p_ref, h0_ref, h0n_ref) + y, g1_ref[...], b1_ref[...])
    h1_ref[...] = h1e[HALO:HALO + rows]
    prev = h1e[:HALO].astype(BF16)
    nxt = h1e[HALO + rows:].astype(BF16)
    hext_ref[:HALO] = jnp.where(i == 0, jnp.zeros_like(prev), prev)
    hext_ref[HALO:HALO + rows] = h1e[HALO:HALO + rows].astype(BF16)
    hext_ref[HALO + rows:] = jnp.where(i == steps_per_seq - 1, jnp.zeros_like(nxt), nxt)

    def columns(c):
        gate = pl.ds(pl.multiple_of(c * FFN_COLS, FFN_COLS), FFN_COLS)
        val = pl.ds(pl.multiple_of(D_FF + c * FFN_COLS, FFN_COLS), FFN_COLS)
        return gate, val

    def stage(up, down, first=False):
        ext_rows = rows + 2 * HALO
        up_rows = ext_rows // _FFN_SPLIT
        down_rows = rows // (2 * _FFN_SPLIT)
        if up is not None:
            up_cols = columns(up[0])

            def project(piece):
                half, part = divmod(piece, _FFN_SPLIT)
                r0 = part * up_rows
                up[1][half, r0:r0 + up_rows, :] = _dot(hext_ref[r0:r0 + up_rows, :],
                                                       wup_ref[:, up_cols[half]])

        if down is not None:
            c, a_ref = down
            down_cols = columns(c)
            w = wdn_ref[pl.ds(pl.multiple_of(c * FFN_COLS, FFN_COLS), FFN_COLS), :]

            def conv(half, r0):
                cols = down_cols[half]
                cw = cw_ref[:, cols]
                lo = HALO + r0
                return (cw[0:1] * a_ref[half, lo - 1:lo - 1 + down_rows, :]
                        + cw[1:2] * a_ref[half, lo:lo + down_rows, :]
                        + cw[2:3] * a_ref[half, lo + 1:lo + 1 + down_rows, :]
                        + cb_ref[:, cols])

            def contract(piece):
                r0 = piece * down_rows
                act = (_gelu(conv(0, r0)) * conv(1, r0)).astype(BF16)
                if first:
                    f_ref[r0:r0 + down_rows, :] = _dot(act, w)
                else:
                    f_ref[r0:r0 + down_rows, :] += _dot(act, w)

        for piece in range(2 * _FFN_SPLIT):
            if up is not None:
                project(piece)
            if down is not None:
                contract(piece)

    stage((0, a0_ref), None)
    stage((1, a1_ref), (0, a0_ref), first=True)

    def group(p, carry):
        c = _FFN_UNROLL * p + 1
        for u in range(0, _FFN_UNROLL, 2):
            stage((c + u + 1, a0_ref), (c + u, a1_ref))
            stage((c + u + 2, a1_ref), (c + u + 1, a0_ref))
        return carry

    lax.fori_loop(0, (n_chunks - 3) // _FFN_UNROLL, group, 0)
    stage((n_chunks - 1, a0_ref), (n_chunks - 2, a1_ref))
    stage(None, (n_chunks - 1, a0_ref))
    o_ref[...] = _layer_norm(ALPHA * h1_ref[...] + f_ref[...], g_ref[...], b_ref[...])


def _proj_ffn(h0, yg, yd, ym, w_out, g1, b1, w_up, conv_w, conv_b, w_down, g2, b2, seq_len):
    N, D = h0.shape
    R = FFN_ROWS
    halo_per_step = R // HALO
    n_halo = N // HALO
    n_chunks = D_FF // FFN_COLS
    assert n_chunks >= 3 and (n_chunks - 3) % _FFN_UNROLL == 0
    const = lambda shape: pl.BlockSpec(shape, lambda i: (0, 0))

    def with_halo(width):
        return [
            pl.BlockSpec((HALO, width), lambda i: (jnp.maximum(i * halo_per_step - 1, 0), 0)),
            pl.BlockSpec((R, width), lambda i: (i, 0)),
            pl.BlockSpec((HALO, width),
                         lambda i: (jnp.minimum((i + 1) * halo_per_step, n_halo - 1), 0)),
        ]

    return pl.pallas_call(
        functools.partial(_proj_ffn_kernel, steps_per_seq=seq_len // R),
        grid=(N // R,),
        in_specs=(with_halo(D) + with_halo(G_WIDTH) + with_halo(DA_WIDTH) + with_halo(M_WIDTH)
                  + [const((D, D)), const((1, D)), const((1, D)),
                     const((D, 2 * D_FF)), const((CONV_W, 2 * D_FF)), const((1, 2 * D_FF)),
                     const((D_FF, D)), const((1, D)), const((1, D))]),
        out_specs=pl.BlockSpec((R, D), lambda i: (i, 0)),
        out_shape=jax.ShapeDtypeStruct((N, D), F32),
        scratch_shapes=[
            pltpu.VMEM((R + 2 * HALO, D), BF16),
            pltpu.VMEM((R, D), F32),
            pltpu.VMEM((2, R + 2 * HALO, FFN_COLS), F32),
            pltpu.VMEM((2, R + 2 * HALO, FFN_COLS), F32),
            pltpu.VMEM((R, D), F32),
        ],
        compiler_params=_params(("arbitrary",)),
        name="proj_ffn",
    )(h0, h0, h0, yg, yg, yg, yd, yd, yd, ym, ym, ym, w_out, g1, b1,
      w_up, conv_w, conv_b, w_down, g2, b2)


def _alibi_slopes(n):
    return jnp.asarray([2.0 ** (-8.0 * (i + 1) / n) for i in range(n)], F32)


def kernel(x, mem, ln_emb_g, ln_emb_b, w_in, gmlp_ln_g, gmlp_ln_b, gmlp_ws, gmlp_bs, lambda_q1, lambda_k1, lambda_q2, lambda_k2, da_subln_g, mem_ln_g, mem_ln_b, w_mem_kv, w_out, ln1_g, ln1_b, w_up, conv_w, conv_b, w_down, ln2_g, ln2_b):
    B, T, D = x.shape
    assert DEPTH == 1 and w_in.shape[0] == 1
    assert T % PROJ_ROWS == 0 and T % ATT_Q == 0 and T % ATT_K == 0 and T % FFN_ROWS == 0
    assert D_FF % FFN_COLS == 0
    l = 0
    lambda_init = 0.8 - 0.6 * math.exp(-0.3 * l)
    row = lambda a: a.reshape(1, -1)

    w = w_in[l]
    u_end, v_end = G_WIDTH, 2 * G_WIDTH
    q_end = v_end + DA_QK
    k_end = q_end + DA_QK
    vd_end = k_end + DA_WIDTH
    w_a = jnp.concatenate([w[:, :v_end], w[:, q_end:k_end], w[:, vd_end:]], axis=1).astype(BF16)
    w_bT = jnp.concatenate([w[:, v_end:q_end], w[:, k_end:vd_end]], axis=1).T.astype(BF16)
    w_kmT = w_mem_kv[l][:, :M_WIDTH].T.astype(BF16)
    w_vm = w_mem_kv[l][:, M_WIDTH:].astype(BF16)
    ws = gmlp_ws[l].reshape(G_HEADS * CHUNK, CHUNK).astype(BF16)
    bs_tile = jnp.repeat(gmlp_bs[l].T, G_DIM, axis=1)

    kmT, vm = _mem_kv(mem, row(mem_ln_g[l]), row(mem_ln_b[l]), w_kmT, w_vm)
    h0, yg, ym, qT, k, vT = _in_proj(
        x, row(ln_emb_g), row(ln_emb_b), w_a, w_bT, row(gmlp_ln_g[l]), row(gmlp_ln_b[l]),
        ws, bs_tile, kmT, vm)
    yd = _diff_attn(_alibi_slopes(DA_HEADS), qT, k, vT, row(lambda_q1[l]), row(lambda_k1[l]),
                    row(lambda_q2[l]), row(lambda_k2[l]), row(da_subln_g[l]), lambda_init)

    N = B * T
    out = _proj_ffn(h0.reshape(N, D), yg.reshape(N, G_WIDTH), yd.reshape(N, DA_WIDTH),
                    ym.reshape(N, M_WIDTH), w_out[l].astype(BF16), row(ln1_g[l]), row(ln1_b[l]),
                    w_up[l].astype(BF16), conv_w[l], row(conv_b[l]), w_down[l].astype(BF16),
                    row(ln2_g[l]), row(ln2_b[l]), T)
    return out.reshape(B, T, D)
```

```python
import functools
import math

import jax
import jax.numpy as jnp
import numpy as np
from jax import lax
from jax.experimental import pallas as pl
from jax.experimental.pallas import tpu as pltpu

F32 = jnp.float32
BF16 = jnp.bfloat16

D_MODEL = 1024
DEPTH = 1
MEM_LEN = 256
CHUNK = 128
G_HEADS = 4
G_WIDTH = D_MODEL // 4
G_DIM = G_WIDTH // G_HEADS
DA_HEADS = 4
DA_WIDTH = D_MODEL // 2
DA_V_DIM = DA_WIDTH // DA_HEADS
DA_HEAD_DIM = DA_V_DIM // 2
DA_QK = DA_HEADS * 2 * DA_HEAD_DIM
M_HEADS = 4
M_WIDTH = D_MODEL // 4
M_HEAD_DIM = M_WIDTH // M_HEADS
D_FF = 2816
CONV_W = 3
LN_EPS = 1e-5
ALPHA = (2.0 * DEPTH) ** 0.25
LOG2E = math.log2(math.e)

V7X_VMEM_LIMIT_BYTES = 56 * 1024 * 1024
BF16_SUBLANE_TILE = 16

PROJ_ROWS = 512
ATT_Q = 1024
ATT_K = 512
FFN_ROWS = 512
FFN_COLS = 256
HALO = BF16_SUBLANE_TILE

NT_DIMS = (((1,), (1,)), ((), ()))


def _layer_norm(x, g, b):
    mu = jnp.mean(x, axis=-1, keepdims=True)
    xc = x - mu
    var = jnp.mean(xc * xc, axis=-1, keepdims=True)
    return xc * lax.rsqrt(var + LN_EPS) * g + b


def _gelu(x):
    return 0.5 * x * (1.0 + lax.erf(x * (2.0 ** -0.5)))


def _dot(a, b):
    return jnp.dot(a, b, preferred_element_type=F32)


def _params(semantics):
    return pltpu.CompilerParams(dimension_semantics=semantics,
                                vmem_limit_bytes=V7X_VMEM_LIMIT_BYTES)


def _mem_kv_kernel(mem_ref, g_ref, b_ref, wkT_ref, wv_ref, kmT_ref, vm_ref):
    m = _layer_norm(mem_ref[...], g_ref[...], b_ref[...]).astype(BF16)
    kT = lax.dot_general(wkT_ref[...], m, NT_DIMS, preferred_element_type=F32)
    kT = kT * (M_HEAD_DIM ** -0.5)
    v = _dot(m, wv_ref[...])
    head_of_row = lax.broadcasted_iota(jnp.int32, (M_WIDTH, MEM_LEN), 0) // M_HEAD_DIM
    head_of_col = lax.broadcasted_iota(jnp.int32, (MEM_LEN, M_WIDTH), 1) // M_HEAD_DIM
    for h in range(M_HEADS):
        kmT_ref[h] = jnp.where(head_of_row == h, kT, 0.0).astype(BF16)
        vm_ref[h] = jnp.where(head_of_col == h, v, 0.0).astype(BF16)


def _mem_kv(mem, g, b, wkT, wv):
    B = mem.shape[0]
    row = lambda: pl.BlockSpec((1, D_MODEL), lambda i: (0, 0))
    return pl.pallas_call(
        _mem_kv_kernel,
        grid=(B,),
        in_specs=[
            pl.BlockSpec((None, MEM_LEN, D_MODEL), lambda i: (i, 0, 0)),
            row(), row(),
            pl.BlockSpec((M_WIDTH, D_MODEL), lambda i: (0, 0)),
            pl.BlockSpec((D_MODEL, M_WIDTH), lambda i: (0, 0)),
        ],
        out_specs=[
            pl.BlockSpec((None, M_HEADS, M_WIDTH, MEM_LEN), lambda i: (i, 0, 0, 0)),
            pl.BlockSpec((None, M_HEADS, MEM_LEN, M_WIDTH), lambda i: (i, 0, 0, 0)),
        ],
        out_shape=[
            jax.ShapeDtypeStruct((B, M_HEADS, M_WIDTH, MEM_LEN), BF16),
            jax.ShapeDtypeStruct((B, M_HEADS, MEM_LEN, M_WIDTH), BF16),
        ],
        compiler_params=_params(("arbitrary",)),
        name="mem_kv",
    )(mem, g, b, wkT, wv)


_A_U = 0
_A_V = G_WIDTH
_A_K = 2 * G_WIDTH
_A_QM = 2 * G_WIDTH + DA_QK
_A_END = _A_QM + M_WIDTH


def _in_proj_kernel(x_ref, eg_ref, eb_ref, wa_ref, wbT_ref, gg_ref, gb_ref, ws_ref, bs_ref,
                    kmT_ref, vm_ref,
                    h0_ref, yg_ref, ym_ref, qT_ref, k_ref, vT_ref):
    rows = x_ref.shape[0]
    h = _layer_norm(x_ref[...], eg_ref[...], eb_ref[...])
    h0_ref[...] = h
    hb = h.astype(BF16)

    pa = _dot(hb, wa_ref[...])
    pbT = lax.dot_general(wbT_ref[...], hb, NT_DIMS, preferred_element_type=F32)
    qT_ref[...] = (pbT[:DA_QK] * (DA_HEAD_DIM ** -0.5 * LOG2E)).astype(BF16)
    vT_ref[...] = pbT[DA_QK:].astype(BF16)
    k_ref[...] = pa[:, _A_K:_A_QM].astype(BF16)

    u = _gelu(pa[:, _A_U:_A_V])
    v = _layer_norm(_gelu(pa[:, _A_V:_A_K]), gg_ref[...], gb_ref[...]).astype(BF16)
    group_of_lane = lax.broadcasted_iota(jnp.int32, (CHUNK, G_WIDTH), 1) // G_DIM
    ws = ws_ref[...]
    bs = bs_ref[...]
    for c in range(rows // CHUNK):
        sl = slice(c * CHUNK, (c + 1) * CHUNK)
        r = _dot(ws, v[sl])
        s = r[(G_HEADS - 1) * CHUNK:]
        for g in range(G_HEADS - 2, -1, -1):
            s = jnp.where(group_of_lane == g, r[g * CHUNK:(g + 1) * CHUNK], s)
        yg_ref[sl, :] = (u[sl] * (s + bs)).astype(BF16)

    qm = pa[:, _A_QM:_A_END].astype(BF16)
    o = jnp.zeros((rows, M_WIDTH), F32)
    for hd in range(M_HEADS):
        s = _dot(qm, kmT_ref[hd])
        p = jnp.exp(s - jnp.max(s, axis=-1, keepdims=True))
        p = p / jnp.sum(p, axis=-1, keepdims=True)
        o = o + _dot(p.astype(BF16), vm_ref[hd])
    ym_ref[...] = o.astype(BF16)


def _in_proj(x, eg, eb, w_a, w_bT, gg, gb, ws, bs_tile, kmT, vm):
    B, T, D = x.shape
    R = PROJ_ROWS
    const2 = lambda shape: pl.BlockSpec(shape, lambda b, i: (0, 0))
    return pl.pallas_call(
        _in_proj_kernel,
        grid=(B, T // R),
        in_specs=[
            pl.BlockSpec((None, R, D), lambda b, i: (b, i, 0)),
            const2((1, D)), const2((1, D)),
            const2((D, _A_END)),
            const2((DA_QK + DA_WIDTH, D)),
            const2((1, G_WIDTH)), const2((1, G_WIDTH)),
            const2((G_HEADS * CHUNK, CHUNK)),
            const2((CHUNK, G_WIDTH)),
            pl.BlockSpec((None, M_HEADS, M_WIDTH, MEM_LEN), lambda b, i: (b, 0, 0, 0)),
            pl.BlockSpec((None, M_HEADS, MEM_LEN, M_WIDTH), lambda b, i: (b, 0, 0, 0)),
        ],
        out_specs=[
            pl.BlockSpec((None, R, D), lambda b, i: (b, i, 0)),
            pl.BlockSpec((None, R, G_WIDTH), lambda b, i: (b, i, 0)),
            pl.BlockSpec((None, R, M_WIDTH), lambda b, i: (b, i, 0)),
            pl.BlockSpec((None, DA_QK, R), lambda b, i: (b, 0, i)),
            pl.BlockSpec((None, R, DA_QK), lambda b, i: (b, i, 0)),
            pl.BlockSpec((None, DA_WIDTH, R), lambda b, i: (b, 0, i)),
        ],
        out_shape=[
            jax.ShapeDtypeStruct((B, T, D), F32),
            jax.ShapeDtypeStruct((B, T, G_WIDTH), BF16),
            jax.ShapeDtypeStruct((B, T, M_WIDTH), BF16),
            jax.ShapeDtypeStruct((B, DA_QK, T), BF16),
            jax.ShapeDtypeStruct((B, T, DA_QK), BF16),
            jax.ShapeDtypeStruct((B, DA_WIDTH, T), BF16),
        ],
        compiler_params=_params(("arbitrary", "arbitrary")),
        name="in_proj",
    )(x, eg, eb, w_a, w_bT, gg, gb, ws, bs_tile, kmT, vm)


_POS_SPLIT = 64
_BIAS_ROWS = BF16_SUBLANE_TILE
_ACC_ROWS = DA_V_DIM + BF16_SUBLANE_TILE


def _bf16_split3(x):
    a = x.astype(BF16).astype(F32)
    b = (x - a).astype(BF16).astype(F32)
    c = x - a - b
    return a, b, c


def _diff_attn_kernel(slopes_ref, qT_ref, k_ref, pos_ref, vT_ref, lq1_ref, lk1_ref, lq2_ref,
                      lk2_ref, g_ref, o_ref, qa_ref, base_ref, sa_ref, sb_ref, bma_ref, bmb_ref,
                      m_ref, acc_ref, diag_ref, *, lambda_init):
    tq = ATT_Q
    tk = ATT_K
    two_dk = 2 * DA_HEAD_DIM
    nk = k_ref.shape[0] // tk
    nq = qT_ref.shape[1] // tq
    sub = tq // tk
    c = slopes_ref[pl.program_id(1)] * LOG2E

    kk = lax.broadcasted_iota(jnp.int32, (tk, tk), 0)
    qq = lax.broadcasted_iota(jnp.int32, (tk, tk), 1)
    diag_ref[...] = -c * jnp.abs(kk - qq).astype(F32)

    lane = lax.broadcasted_iota(jnp.int32, (1, 2 * tq), 1)
    col = jnp.where(lane >= tq, lane - tq, lane)
    group = col // tk

    lam = (jnp.exp(jnp.sum(lq1_ref[...] * lk1_ref[...], axis=-1, keepdims=True))
           - jnp.exp(jnp.sum(lq2_ref[...] * lk2_ref[...], axis=-1, keepdims=True))
           + lambda_init)
    ones_row = jnp.where(lax.broadcasted_iota(jnp.int32, (BF16_SUBLANE_TILE, tk), 0) == 0,
                         1.0, 0.0).astype(BF16)
    qa_ref[two_dk + _BIAS_ROWS:, :] = jnp.zeros((two_dk - _BIAS_ROWS, 2 * tq), BF16)

    def load_queries(qn):
        qt = qT_ref[:, pl.ds(pl.multiple_of(qn * tq, tq), tq)]
        zero = jnp.zeros((DA_HEAD_DIM, tq), BF16)
        qa_ref[:DA_HEAD_DIM, :] = jnp.concatenate([qt[:DA_HEAD_DIM], zero], axis=1)
        qa_ref[DA_HEAD_DIM:two_dk, :] = jnp.concatenate([zero, qt[DA_HEAD_DIM:]], axis=1)
        qpos = (qn * tq + col).astype(F32)
        c1, c2, c3 = _bf16_split3(jnp.full((1, 2 * tq), c, F32))
        v1, v2, v3 = _bf16_split3(c * qpos)
        r = lax.broadcasted_iota(jnp.int32, (_BIAS_ROWS, 2 * tq), 0)
        base = jnp.zeros((_BIAS_ROWS, 2 * tq), F32)
        for idx, val in enumerate((-c1, -c2, -c3, -c1, -c2, -c3, v1, v2, v3)):
            base = jnp.where(r == idx, val, base)
        base_ref[...] = base

    def block_start(qn, t):
        return pl.multiple_of(lax.rem(qn * sub + t, nk) * tk, tk)

    n_strips = 2 * sub

    def step(score_args, consume_args):
        if score_args is not None:
            qn, t, s_ref, bm_ref, straddles = score_args
            start = block_start(qn, t)
            if straddles:
                sign = jnp.where(group < t, 1.0, jnp.where(group > t, -1.0, 0.0)).astype(F32)
            else:
                sign = jnp.where(start > qn * tq, 1.0, -1.0).astype(F32)
            qa_ref[two_dk:two_dk + _BIAS_ROWS, :] = (sign * base_ref[...]).astype(BF16)
            kb = jnp.concatenate([k_ref[pl.ds(start, tk), :], pos_ref[pl.ds(start, tk), :]],
                                 axis=1)
        if consume_args is not None:
            cqn, ct, cs_ref, cbm_ref = consume_args
            vb = jnp.concatenate([vT_ref[:, pl.ds(block_start(cqn, ct), tk)], ones_row], axis=0)
        for j in range(n_strips):
            cols = slice(j * tk, (j + 1) * tk)
            if score_args is not None:
                s = _dot(kb, qa_ref[:, cols])
                if straddles and j % sub == t:
                    s = s + diag_ref[...]
                s_ref[:, cols] = s
                bm_ref[:, cols] = jnp.max(s, axis=0, keepdims=True)
            if consume_args is not None:
                m_old = m_ref[:, cols]
                m_new = jnp.maximum(m_old, cbm_ref[:, cols])
                alpha = jnp.exp2(m_old - m_new)
                m_ref[:, cols] = m_new
                p = jnp.exp2(cs_ref[:, cols] - m_new).astype(BF16)
                acc_ref[:, cols] = alpha * acc_ref[:, cols] + _dot(vb, p)

    def reset_stats():
        acc_ref[...] = jnp.zeros_like(acc_ref)
        m_ref[...] = jnp.full(m_ref.shape, -jnp.inf, F32)

    def finish(qn):
        inv_l = 1.0 / acc_ref[DA_V_DIM:DA_V_DIM + 1, :]
        oT = (acc_ref[:DA_V_DIM, :tq] * inv_l[:, :tq]
              - lam * (acc_ref[:DA_V_DIM, tq:] * inv_l[:, tq:]))
        o = oT.T
        ms = jnp.mean(o * o, axis=-1, keepdims=True)
        o = o * lax.rsqrt(ms + LN_EPS) * g_ref[...] * (1.0 - lambda_init)
        o_ref[pl.ds(pl.multiple_of(qn * tq, tq), tq), :] = o.astype(o_ref.dtype)

    bufs = ((sa_ref, bma_ref), (sb_ref, bmb_ref))

    last = (nk - 1) % 2

    def head_rest(qn):
        for t in range(1, sub):
            step((qn, t) + bufs[t % 2] + (True,), (qn, t - 1) + bufs[(t - 1) % 2])

    def body(qn):
        def pair(i, inner):
            t = sub + 2 * i
            step((qn, t) + bufs[sub % 2] + (False,), (qn, t - 1) + bufs[(sub - 1) % 2])
            step((qn, t + 1) + bufs[(sub + 1) % 2] + (False,), (qn, t) + bufs[sub % 2])
            return inner

        for i in range((nk - sub) // 2):
            pair(i, 0)

    load_queries(0)
    step((0, 0) + bufs[0] + (True,), None)
    reset_stats()
    head_rest(0)

    def query_block(qn, carry):
        body(qn)
        load_queries(qn + 1)
        step((qn + 1, 0) + bufs[0] + (True,), (qn, nk - 1) + bufs[last])
        finish(qn)
        reset_stats()
        head_rest(qn + 1)
        return carry

    lax.fori_loop(0, nq - 1, query_block, 0)
    body(nq - 1)
    step(None, (nq - 1, nk - 1) + bufs[last])
    finish(nq - 1)


def _key_positions(T):
    t = np.arange(T)
    hi = (t // _POS_SPLIT) * _POS_SPLIT
    lo = t % _POS_SPLIT
    one = np.ones_like(t)
    cols = np.zeros((T, 2 * DA_HEAD_DIM), np.float32)
    cols[:, :9] = np.stack([hi, hi, hi, lo, lo, lo, one, one, one], axis=1)
    return jnp.asarray(cols, dtype=BF16)


def _diff_attn(slopes, qT, k, vT, lq1, lk1, lq2, lk2, g, lambda_init):
    B, _, T = qT.shape
    two_dk = 2 * DA_HEAD_DIM
    assert ATT_Q % ATT_K == 0 and T % ATT_Q == 0 and T // ATT_Q >= 2
    assert (T // ATT_K - ATT_Q // ATT_K) % 2 == 0
    vec = lambda: pl.BlockSpec((1, DA_HEAD_DIM), lambda b, h: (0, 0))
    return pl.pallas_call(
        functools.partial(_diff_attn_kernel, lambda_init=lambda_init),
        grid=(B, DA_HEADS),
        in_specs=[
            pl.BlockSpec(memory_space=pltpu.SMEM),
            pl.BlockSpec((None, two_dk, T), lambda b, h: (b, h, 0)),
            pl.BlockSpec((None, T, two_dk), lambda b, h: (b, 0, h)),
            pl.BlockSpec((T, two_dk), lambda b, h: (0, 0)),
            pl.BlockSpec((None, DA_V_DIM, T), lambda b, h: (b, h, 0)),
            vec(), vec(), vec(), vec(),
            pl.BlockSpec((1, DA_V_DIM), lambda b, h: (0, 0)),
        ],
        out_specs=pl.BlockSpec((None, T, DA_V_DIM), lambda b, h: (b, 0, h)),
        out_shape=jax.ShapeDtypeStruct((B, T, DA_WIDTH), BF16),
        scratch_shapes=[
            pltpu.VMEM((2 * two_dk, 2 * ATT_Q), BF16),
            pltpu.VMEM((_BIAS_ROWS, 2 * ATT_Q), F32),
            pltpu.VMEM((ATT_K, 2 * ATT_Q), F32),
            pltpu.VMEM((ATT_K, 2 * ATT_Q), F32),
            pltpu.VMEM((1, 2 * ATT_Q), F32),
            pltpu.VMEM((1, 2 * ATT_Q), F32),
            pltpu.VMEM((1, 2 * ATT_Q), F32),
            pltpu.VMEM((_ACC_ROWS, 2 * ATT_Q), F32),
            pltpu.VMEM((ATT_K, ATT_K), F32),
        ],
        compiler_params=_params(("arbitrary", "arbitrary")),
        name="diff_attn",
    )(slopes, qT, k, _key_positions(T), vT, lq1, lk1, lq2, lk2, g)


_FFN_SPLIT = 2
_FFN_UNROLL = 8


def _proj_ffn_kernel(h0p_ref, h0_ref, h0n_ref, ygp_ref, yg_ref, ygn_ref, ydp_ref, yd_ref, ydn_ref,
                     ymp_ref, ym_ref, ymn_ref, wout_ref, g1_ref, b1_ref,
                     wup_ref, cw_ref, cb_ref, wdn_ref, g_ref, b_ref, o_ref,
                     hext_ref, h1_ref, a0_ref, a1_ref, f_ref, *, steps_per_seq):
    rows = h0_ref.shape[0]
    n_chunks = wdn_ref.shape[0] // FFN_COLS
    i = pl.program_id(0) % steps_per_seq

    def with_halo(prev_ref, main_ref, next_ref):
        return jnp.concatenate([prev_ref[...], main_ref[...], next_ref[...]], axis=0)

    y = (_dot(with_halo(ygp_ref, yg_ref, ygn_ref), wout_ref[:G_WIDTH])
         + _dot(with_halo(ydp_ref, yd_ref, ydn_ref), wout_ref[G_WIDTH:G_WIDTH + DA_WIDTH])
         + _dot(with_halo(ymp_ref, ym_ref, ymn_ref), wout_ref[G_WIDTH + DA_WIDTH:]))
    h1e = _layer_norm(ALPHA * with_halo(h0p_ref, h0_ref, h0n_ref) + y, g1_ref[...], b1_ref[...])
    h1_ref[...] = h1e[HALO:HALO + rows]
    prev = h1e[:HALO].astype(BF16)
    nxt = h1e[HALO + rows:].astype(BF16)
    hext_ref[:HALO] = jnp.where(i == 0, jnp.zeros_like(prev), prev)
    hext_ref[HALO:HALO + rows] = h1e[HALO:HALO + rows].astype(BF16)
    hext_ref[HALO + rows:] = jnp.where(i == steps_per_seq - 1, jnp.zeros_like(nxt), nxt)

    def columns(c):
        gate = pl.ds(pl.multiple_of(c * FFN_COLS, FFN_COLS), FFN_COLS)
        val = pl.ds(pl.multiple_of(D_FF + c * FFN_COLS, FFN_COLS), FFN_COLS)
        return gate, val

    def stage(up, down, first=False):
        ext_rows = rows + 2 * HALO
        up_rows = ext_rows // _FFN_SPLIT
        down_rows = rows // (2 * _FFN_SPLIT)
        if up is not None:
            up_cols = columns(up[0])

            def project(piece):
                half, part = divmod(piece, _FFN_SPLIT)
                r0 = part * up_rows
                up[1][half, r0:r0 + up_rows, :] = _dot(hext_ref[r0:r0 + up_rows, :],
                                                       wup_ref[:, up_cols[half]])

        if down is not None:
            c, a_ref = down
            down_cols = columns(c)
            w = wdn_ref[pl.ds(pl.multiple_of(c * FFN_COLS, FFN_COLS), FFN_COLS), :]

            def conv(half, r0):
                cols = down_cols[half]
                cw = cw_ref[:, cols]
                lo = HALO + r0
                return (cw[0:1] * a_ref[half, lo - 1:lo - 1 + down_rows, :]
                        + cw[1:2] * a_ref[half, lo:lo + down_rows, :]
                        + cw[2:3] * a_ref[half, lo + 1:lo + 1 + down_rows, :]
                        + cb_ref[:, cols])

            def contract(piece):
                r0 = piece * down_rows
                act = (_gelu(conv(0, r0)) * conv(1, r0)).astype(BF16)
                if first:
                    f_ref[r0:r0 + down_rows, :] = _dot(act, w)
                else:
                    f_ref[r0:r0 + down_rows, :] += _dot(act, w)

        for piece in range(2 * _FFN_SPLIT):
            if up is not None:
                project(piece)
            if down is not None:
                contract(piece)

    stage((0, a0_ref), None)
    stage((1, a1_ref), (0, a0_ref), first=True)

    def group(p, carry):
        c = _FFN_UNROLL * p + 1
        for u in range(0, _FFN_UNROLL, 2):
            stage((c + u + 1, a0_ref), (c + u, a1_ref))
            stage((c + u + 2, a1_ref), (c + u + 1, a0_ref))
        return carry

    lax.fori_loop(0, (n_chunks - 3) // _FFN_UNROLL, group, 0)
    stage((n_chunks - 1, a0_ref), (n_chunks - 2, a1_ref))
    stage(None, (n_chunks - 1, a0_ref))
    o_ref[...] = _layer_norm(ALPHA * h1_ref[...] + f_ref[...], g_ref[...], b_ref[...])


def _proj_ffn(h0, yg, yd, ym, w_out, g1, b1, w_up, conv_w, conv_b, w_down, g2, b2, seq_len):
    N, D = h0.shape
    R = FFN_ROWS
    halo_per_step = R // HALO
    n_halo = N // HALO
    n_chunks = D_FF // FFN_COLS
    assert n_chunks >= 3 and (n_chunks - 3) % _FFN_UNROLL == 0
    const = lambda shape: pl.BlockSpec(shape, lambda i: (0, 0))

    def with_halo(width):
        return [
            pl.BlockSpec((HALO, width), lambda i: (jnp.maximum(i * halo_per_step - 1, 0), 0)),
            pl.BlockSpec((R, width), lambda i: (i, 0)),
            pl.BlockSpec((HALO, width),
                         lambda i: (jnp.minimum((i + 1) * halo_per_step, n_halo - 1), 0)),
        ]

    return pl.pallas_call(
        functools.partial(_proj_ffn_kernel, steps_per_seq=seq_len // R),
        grid=(N // R,),
        in_specs=(with_halo(D) + with_halo(G_WIDTH) + with_halo(DA_WIDTH) + with_halo(M_WIDTH)
                  + [const((D, D)), const((1, D)), const((1, D)),
                     const((D, 2 * D_FF)), const((CONV_W, 2 * D_FF)), const((1, 2 * D_FF)),
                     const((D_FF, D)), const((1, D)), const((1, D))]),
        out_specs=pl.BlockSpec((R, D), lambda i: (i, 0)),
        out_shape=jax.ShapeDtypeStruct((N, D), F32),
        scratch_shapes=[
            pltpu.VMEM((R + 2 * HALO, D), BF16),
            pltpu.VMEM((R, D), F32),
            pltpu.VMEM((2, R + 2 * HALO, FFN_COLS), F32),
            pltpu.VMEM((2, R + 2 * HALO, FFN_COLS), F32),
            pltpu.VMEM((R, D), F32),
        ],
        compiler_params=_params(("arbitrary",)),
        name="proj_ffn",
    )(h0, h0, h0, yg, yg, yg, yd, yd, yd, ym, ym, ym, w_out, g1, b1,
      w_up, conv_w, conv_b, w_down, g2, b2)


def _alibi_slopes(n):
    return jnp.asarray([2.0 ** (-8.0 * (i + 1) / n) for i in range(n)], F32)


def kernel(x, mem, ln_emb_g, ln_emb_b, w_in, gmlp_ln_g, gmlp_ln_b, gmlp_ws, gmlp_bs, lambda_q1, lambda_k1, lambda_q2, lambda_k2, da_subln_g, mem_ln_g, mem_ln_b, w_mem_kv, w_out, ln1_g, ln1_b, w_up, conv_w, conv_b, w_down, ln2_g, ln2_b):
    B, T, D = x.shape
    assert DEPTH == 1 and w_in.shape[0] == 1
    assert T % PROJ_ROWS == 0 and T % ATT_Q == 0 and T % ATT_K == 0 and T % FFN_ROWS == 0
    assert D_FF % FFN_COLS == 0
    l = 0
    lambda_init = 0.8 - 0.6 * math.exp(-0.3 * l)
    row = lambda a: a.reshape(1, -1)

    w = w_in[l]
    u_end, v_end = G_WIDTH, 2 * G_WIDTH
    q_end = v_end + DA_QK
    k_end = q_end + DA_QK
    vd_end = k_end + DA_WIDTH
    w_a = jnp.concatenate([w[:, :v_end], w[:, q_end:k_end], w[:, vd_end:]], axis=1).astype(BF16)
    w_bT = jnp.concatenate([w[:, v_end:q_end], w[:, k_end:vd_end]], axis=1).T.astype(BF16)
    w_kmT = w_mem_kv[l][:, :M_WIDTH].T.astype(BF16)
    w_vm = w_mem_kv[l][:, M_WIDTH:].astype(BF16)
    ws = gmlp_ws[l].reshape(G_HEADS * CHUNK, CHUNK).astype(BF16)
    bs_tile = jnp.repeat(gmlp_bs[l].T, G_DIM, axis=1)

    kmT, vm = _mem_kv(mem, row(mem_ln_g[l]), row(mem_ln_b[l]), w_kmT, w_vm)
    h0, yg, ym, qT, k, vT = _in_proj(
        x, row(ln_emb_g), row(ln_emb_b), w_a, w_bT, row(gmlp_ln_g[l]), row(gmlp_ln_b[l]),
        ws, bs_tile, kmT, vm)
    yd = _diff_attn(_alibi_slopes(DA_HEADS), qT, k, vT, row(lambda_q1[l]), row(lambda_k1[l]),
                    row(lambda_q2[l]), row(lambda_k2[l]), row(da_subln_g[l]), lambda_init)

    N = B * T
    out = _proj_ffn(h0.reshape(N, D), yg.reshape(N, G_WIDTH), yd.reshape(N, DA_WIDTH),
                    ym.reshape(N, M_WIDTH), w_out[l].astype(BF16), row(ln1_g[l]), row(ln1_b[l]),
                    w_up[l].astype(BF16), conv_w[l], row(conv_b[l]), w_down[l].astype(BF16),
                    row(ln2_g[l]), row(ln2_b[l]), T)
    return out.reshape(B, T, D)
```

```python
import functools
import math

import jax
import jax.numpy as jnp
import numpy as np
from jax import lax
from jax.experimental import pallas as pl
from jax.experimental.pallas import tpu as pltpu

F32 = jnp.float32
BF16 = jnp.bfloat16

D_MODEL = 1024
DEPTH = 1
MEM_LEN = 256
CHUNK = 128
G_HEADS = 4
G_WIDTH = D_MODEL // 4
G_DIM = G_WIDTH // G_HEADS
DA_HEADS = 4
DA_WIDTH = D_MODEL // 2
DA_V_DIM = DA_WIDTH // DA_HEADS
DA_HEAD_DIM = DA_V_DIM // 2
DA_QK = DA_HEADS * 2 * DA_HEAD_DIM
M_HEADS = 4
M_WIDTH = D_MODEL // 4
M_HEAD_DIM = M_WIDTH // M_HEADS
D_FF = 2816
CONV_W = 3
LN_EPS = 1e-5
ALPHA = (2.0 * DEPTH) ** 0.25
LOG2E = math.log2(math.e)

V7X_VMEM_LIMIT_BYTES = 56 * 1024 * 1024
BF16_SUBLANE_TILE = 16

PROJ_ROWS = 512
ATT_Q = 1024
ATT_K = 512
FFN_ROWS = 512
FFN_COLS = 256
HALO = BF16_SUBLANE_TILE

NT_DIMS = (((1,), (1,)), ((), ()))


def _layer_norm(x, g, b):
    mu = jnp.mean(x, axis=-1, keepdims=True)
    xc = x - mu
    var = jnp.mean(xc * xc, axis=-1, keepdims=True)
    return xc * lax.rsqrt(var + LN_EPS) * g + b


def _gelu(x):
    return 0.5 * x * (1.0 + lax.erf(x * (2.0 ** -0.5)))


def _dot(a, b):
    return jnp.dot(a, b, preferred_element_type=F32)


def _params(semantics):
    return pltpu.CompilerParams(dimension_semantics=semantics,
                                vmem_limit_bytes=V7X_VMEM_LIMIT_BYTES)


def _mem_kv_kernel(mem_ref, g_ref, b_ref, wkT_ref, wv_ref, kmT_ref, vm_ref):
    m = _layer_norm(mem_ref[...], g_ref[...], b_ref[...]).astype(BF16)
    kT = lax.dot_general(wkT_ref[...], m, NT_DIMS, preferred_element_type=F32)
    kT = kT * (M_HEAD_DIM ** -0.5)
    v = _dot(m, wv_ref[...])
    head_of_row = lax.broadcasted_iota(jnp.int32, (M_WIDTH, MEM_LEN), 0) // M_HEAD_DIM
    head_of_col = lax.broadcasted_iota(jnp.int32, (MEM_LEN, M_WIDTH), 1) // M_HEAD_DIM
    for h in range(M_HEADS):
        kmT_ref[h] = jnp.where(head_of_row == h, kT, 0.0).astype(BF16)
        vm_ref[h] = jnp.where(head_of_col == h, v, 0.0).astype(BF16)


_A_U = 0
_A_V = G_WIDTH
_A_K = 2 * G_WIDTH
_A_QM = 2 * G_WIDTH + DA_QK
_A_END = _A_QM + M_WIDTH


def _in_proj_kernel(x_ref, eg_ref, eb_ref, wa_ref, wbT_ref, gg_ref, gb_ref, ws_ref, bs_ref,
                    mem_ref, mg_ref, mb_ref, wkT_ref, wv_ref,
                    h0_ref, yg_ref, ym_ref, qT_ref, k_ref, vT_ref, kmT_ref, vm_ref):
    rows = x_ref.shape[0]

    @pl.when(pl.program_id(1) == 0)
    def _():
        _mem_kv_kernel(mem_ref, mg_ref, mb_ref, wkT_ref, wv_ref, kmT_ref, vm_ref)

    h = _layer_norm(x_ref[...], eg_ref[...], eb_ref[...])
    h0_ref[...] = h
    hb = h.astype(BF16)

    pa = _dot(hb, wa_ref[...])
    pbT = lax.dot_general(wbT_ref[...], hb, NT_DIMS, preferred_element_type=F32)
    qT_ref[...] = (pbT[:DA_QK] * (DA_HEAD_DIM ** -0.5 * LOG2E)).astype(BF16)
    vT_ref[...] = pbT[DA_QK:].astype(BF16)
    k_ref[...] = pa[:, _A_K:_A_QM].astype(BF16)

    u = _gelu(pa[:, _A_U:_A_V])
    v = _layer_norm(_gelu(pa[:, _A_V:_A_K]), gg_ref[...], gb_ref[...]).astype(BF16)
    group_of_lane = lax.broadcasted_iota(jnp.int32, (CHUNK, G_WIDTH), 1) // G_DIM
    ws = ws_ref[...]
    bs = bs_ref[...]
    for c in range(rows // CHUNK):
        sl = slice(c * CHUNK, (c + 1) * CHUNK)
        r = _dot(ws, v[sl])
        s = r[(G_HEADS - 1) * CHUNK:]
        for g in range(G_HEADS - 2, -1, -1):
            s = jnp.where(group_of_lane == g, r[g * CHUNK:(g + 1) * CHUNK], s)
        yg_ref[sl, :] = (u[sl] * (s + bs)).astype(BF16)

    qm = pa[:, _A_QM:_A_END].astype(BF16)
    o = jnp.zeros((rows, M_WIDTH), F32)
    for hd in range(M_HEADS):
        s = _dot(qm, kmT_ref[hd])
        p = jnp.exp(s - jnp.max(s, axis=-1, keepdims=True))
        p = p / jnp.sum(p, axis=-1, keepdims=True)
        o = o + _dot(p.astype(BF16), vm_ref[hd])
    ym_ref[...] = o.astype(BF16)


def _in_proj(x, eg, eb, w_a, w_bT, gg, gb, ws, bs_tile, mem, mg, mb, w_kmT, w_vm):
    B, T, D = x.shape
    R = PROJ_ROWS
    const2 = lambda shape: pl.BlockSpec(shape, lambda b, i: (0, 0))
    return pl.pallas_call(
        _in_proj_kernel,
        grid=(B, T // R),
        in_specs=[
            pl.BlockSpec((None, R, D), lambda b, i: (b, i, 0)),
            const2((1, D)), const2((1, D)),
            const2((D, _A_END)),
            const2((DA_QK + DA_WIDTH, D)),
            const2((1, G_WIDTH)), const2((1, G_WIDTH)),
            const2((G_HEADS * CHUNK, CHUNK)),
            const2((CHUNK, G_WIDTH)),
            pl.BlockSpec((None, MEM_LEN, D), lambda b, i: (b, 0, 0)),
            const2((1, D)), const2((1, D)),
            const2((M_WIDTH, D)),
            const2((D, M_WIDTH)),
        ],
        out_specs=[
            pl.BlockSpec((None, R, D), lambda b, i: (b, i, 0)),
            pl.BlockSpec((None, R, G_WIDTH), lambda b, i: (b, i, 0)),
            pl.BlockSpec((None, R, M_WIDTH), lambda b, i: (b, i, 0)),
            pl.BlockSpec((None, DA_QK, R), lambda b, i: (b, 0, i)),
            pl.BlockSpec((None, R, DA_QK), lambda b, i: (b, i, 0)),
            pl.BlockSpec((None, DA_WIDTH, R), lambda b, i: (b, 0, i)),
        ],
        out_shape=[
            jax.ShapeDtypeStruct((B, T, D), F32),
            jax.ShapeDtypeStruct((B, T, G_WIDTH), BF16),
            jax.ShapeDtypeStruct((B, T, M_WIDTH), BF16),
            jax.ShapeDtypeStruct((B, DA_QK, T), BF16),
            jax.ShapeDtypeStruct((B, T, DA_QK), BF16),
            jax.ShapeDtypeStruct((B, DA_WIDTH, T), BF16),
        ],
        scratch_shapes=[
            pltpu.VMEM((M_HEADS, M_WIDTH, MEM_LEN), BF16),
            pltpu.VMEM((M_HEADS, MEM_LEN, M_WIDTH), BF16),
        ],
        compiler_params=_params(("arbitrary", "arbitrary")),
        name="in_proj",
    )(x, eg, eb, w_a, w_bT, gg, gb, ws, bs_tile, mem, mg, mb, w_kmT, w_vm)


_POS_SPLIT = 64
_BIAS_ROWS = BF16_SUBLANE_TILE
_ACC_ROWS = DA_V_DIM + BF16_SUBLANE_TILE


def _bf16_split3(x):
    a = x.astype(BF16).astype(F32)
    b = (x - a).astype(BF16).astype(F32)
    c = x - a - b
    return a, b, c


def _diff_attn_kernel(slopes_ref, qT_ref, k_ref, pos_ref, vT_ref, lq1_ref, lk1_ref, lq2_ref,
                      lk2_ref, g_ref, o_ref, qa_ref, base_ref, sa_ref, sb_ref, bma_ref, bmb_ref,
                      m_ref, acc_ref, diag_ref, *, lambda_init):
    tq = ATT_Q
    tk = ATT_K
    two_dk = 2 * DA_HEAD_DIM
    nk = k_ref.shape[0] // tk
    nq = qT_ref.shape[1] // tq
    sub = tq // tk
    c = slopes_ref[pl.program_id(1)] * LOG2E

    kk = lax.broadcasted_iota(jnp.int32, (tk, tk), 0)
    qq = lax.broadcasted_iota(jnp.int32, (tk, tk), 1)
    diag_ref[...] = -c * jnp.abs(kk - qq).astype(F32)

    lane = lax.broadcasted_iota(jnp.int32, (1, 2 * tq), 1)
    col = jnp.where(lane >= tq, lane - tq, lane)
    group = col // tk

    lam = (jnp.exp(jnp.sum(lq1_ref[...] * lk1_ref[...], axis=-1, keepdims=True))
           - jnp.exp(jnp.sum(lq2_ref[...] * lk2_ref[...], axis=-1, keepdims=True))
           + lambda_init)
    ones_row = jnp.where(lax.broadcasted_iota(jnp.int32, (BF16_SUBLANE_TILE, tk), 0) == 0,
                         1.0, 0.0).astype(BF16)
    qa_ref[two_dk + _BIAS_ROWS:, :] = jnp.zeros((two_dk - _BIAS_ROWS, 2 * tq), BF16)

    def load_queries(qn):
        qt = qT_ref[:, pl.ds(pl.multiple_of(qn * tq, tq), tq)]
        zero = jnp.zeros((DA_HEAD_DIM, tq), BF16)
        qa_ref[:DA_HEAD_DIM, :] = jnp.concatenate([qt[:DA_HEAD_DIM], zero], axis=1)
        qa_ref[DA_HEAD_DIM:two_dk, :] = jnp.concatenate([zero, qt[DA_HEAD_DIM:]], axis=1)
        qpos = (qn * tq + col).astype(F32)
        c1, c2, c3 = _bf16_split3(jnp.full((1, 2 * tq), c, F32))
        v1, v2, v3 = _bf16_split3(c * qpos)
        r = lax.broadcasted_iota(jnp.int32, (_BIAS_ROWS, 2 * tq), 0)
        base = jnp.zeros((_BIAS_ROWS, 2 * tq), F32)
        for idx, val in enumerate((-c1, -c2, -c3, -c1, -c2, -c3, v1, v2, v3)):
            base = jnp.where(r == idx, val, base)
        base_ref[...] = base

    def block_start(qn, t):
        return pl.multiple_of(lax.rem(qn * sub + t, nk) * tk, tk)

    n_strips = 2 * sub

    def step(score_args, consume_args):
        if score_args is not None:
            qn, t, s_ref, bm_ref, straddles = score_args
            start = block_start(qn, t)
            if straddles:
                sign = jnp.where(group < t, 1.0, jnp.where(group > t, -1.0, 0.0)).astype(F32)
            else:
                sign = jnp.where(start > qn * tq, 1.0, -1.0).astype(F32)
            qa_ref[two_dk:two_dk + _BIAS_ROWS, :] = (sign * base_ref[...]).astype(BF16)
            kb = jnp.concatenate([k_ref[pl.ds(start, tk), :], pos_ref[pl.ds(start, tk), :]],
                                 axis=1)
        if consume_args is not None:
            cqn, ct, cs_ref, cbm_ref = consume_args
            vb = jnp.concatenate([vT_ref[:, pl.ds(block_start(cqn, ct), tk)], ones_row], axis=0)
        for j in range(n_strips):
            cols = slice(j * tk, (j + 1) * tk)
            if score_args is not None:
                s = _dot(kb, qa_ref[:, cols])
                if straddles and j % sub == t:
                    s = s + diag_ref[...]
                s_ref[:, cols] = s
                bm_ref[:, cols] = jnp.max(s, axis=0, keepdims=True)
            if consume_args is not None:
                m_old = m_ref[:, cols]
                m_new = jnp.maximum(m_old, cbm_ref[:, cols])
                alpha = jnp.exp2(m_old - m_new)
                m_ref[:, cols] = m_new
                p = jnp.exp2(cs_ref[:, cols] - m_new).astype(BF16)
                acc_ref[:, cols] = alpha * acc_ref[:, cols] + _dot(vb, p)

    def reset_stats():
        acc_ref[...] = jnp.zeros_like(acc_ref)
        m_ref[...] = jnp.full(m_ref.shape, -jnp.inf, F32)

    def finish(qn):
        inv_l = 1.0 / acc_ref[DA_V_DIM:DA_V_DIM + 1, :]
        oT = (acc_ref[:DA_V_DIM, :tq] * inv_l[:, :tq]
              - lam * (acc_ref[:DA_V_DIM, tq:] * inv_l[:, tq:]))
        o = oT.T
        ms = jnp.mean(o * o, axis=-1, keepdims=True)
        o = o * lax.rsqrt(ms + LN_EPS) * g_ref[...] * (1.0 - lambda_init)
        o_ref[pl.ds(pl.multiple_of(qn * tq, tq), tq), :] = o.astype(o_ref.dtype)

    bufs = ((sa_ref, bma_ref), (sb_ref, bmb_ref))

    last = (nk - 1) % 2

    def head_rest(qn):
        for t in range(1, sub):
            step((qn, t) + bufs[t % 2] + (True,), (qn, t - 1) + bufs[(t - 1) % 2])

    def body(qn):
        def pair(i, inner):
            t = sub + 2 * i
            step((qn, t) + bufs[sub % 2] + (False,), (qn, t - 1) + bufs[(sub - 1) % 2])
            step((qn, t + 1) + bufs[(sub + 1) % 2] + (False,), (qn, t) + bufs[sub % 2])
            return inner

        for i in range((nk - sub) // 2):
            pair(i, 0)

    load_queries(0)
    step((0, 0) + bufs[0] + (True,), None)
    reset_stats()
    head_rest(0)

    def query_block(qn, carry):
        body(qn)
        load_queries(qn + 1)
        step((qn + 1, 0) + bufs[0] + (True,), (qn, nk - 1) + bufs[last])
        finish(qn)
        reset_stats()
        head_rest(qn + 1)
        return carry

    lax.fori_loop(0, nq - 1, query_block, 0)
    body(nq - 1)
    step(None, (nq - 1, nk - 1) + bufs[last])
    finish(nq - 1)


def _key_positions(T):
    t = np.arange(T)
    hi = (t // _POS_SPLIT) * _POS_SPLIT
    lo = t % _POS_SPLIT
    one = np.ones_like(t)
    cols = np.zeros((T, 2 * DA_HEAD_DIM), np.float32)
    cols[:, :9] = np.stack([hi, hi, hi, lo, lo, lo, one, one, one], axis=1)
    return jnp.asarray(cols, dtype=BF16)


def _diff_attn(slopes, qT, k, vT, lq1, lk1, lq2, lk2, g, lambda_init):
    B, _, T = qT.shape
    two_dk = 2 * DA_HEAD_DIM
    assert ATT_Q % ATT_K == 0 and T % ATT_Q == 0 and T // ATT_Q >= 2
    assert (T // ATT_K - ATT_Q // ATT_K) % 2 == 0
    vec = lambda: pl.BlockSpec((1, DA_HEAD_DIM), lambda b, h: (0, 0))
    return pl.pallas_call(
        functools.partial(_diff_attn_kernel, lambda_init=lambda_init),
        grid=(B, DA_HEADS),
        in_specs=[
            pl.BlockSpec(memory_space=pltpu.SMEM),
            pl.BlockSpec((None, two_dk, T), lambda b, h: (b, h, 0)),
            pl.BlockSpec((None, T, two_dk), lambda b, h: (b, 0, h)),
            pl.BlockSpec((T, two_dk), lambda b, h: (0, 0)),
            pl.BlockSpec((None, DA_V_DIM, T), lambda b, h: (b, h, 0)),
            vec(), vec(), vec(), vec(),
            pl.BlockSpec((1, DA_V_DIM), lambda b, h: (0, 0)),
        ],
        out_specs=pl.BlockSpec((None, T, DA_V_DIM), lambda b, h: (b, 0, h)),
        out_shape=jax.ShapeDtypeStruct((B, T, DA_WIDTH), BF16),
        scratch_shapes=[
            pltpu.VMEM((2 * two_dk, 2 * ATT_Q), BF16),
            pltpu.VMEM((_BIAS_ROWS, 2 * ATT_Q), F32),
            pltpu.VMEM((ATT_K, 2 * ATT_Q), F32),
            pltpu.VMEM((ATT_K, 2 * ATT_Q), F32),
            pltpu.VMEM((1, 2 * ATT_Q), F32),
            pltpu.VMEM((1, 2 * ATT_Q), F32),
            pltpu.VMEM((1, 2 * ATT_Q), F32),
            pltpu.VMEM((_ACC_ROWS, 2 * ATT_Q), F32),
            pltpu.VMEM((ATT_K, ATT_K), F32),
        ],
        compiler_params=_params(("arbitrary", "arbitrary")),
        name="diff_attn",
    )(slopes, qT, k, _key_positions(T), vT, lq1, lk1, lq2, lk2, g)


_FFN_SPLIT = 2
_FFN_UNROLL = 8


def _proj_ffn_kernel(h0p_ref, h0_ref, h0n_ref, ygp_ref, yg_ref, ygn_ref, ydp_ref, yd_ref, ydn_ref,
                     ymp_ref, ym_ref, ymn_ref, wout_ref, g1_ref, b1_ref,
                     wup_ref, cw_ref, cb_ref, wdn_ref, g_ref, b_ref, o_ref,
                     hext_ref, h1_ref, a0_ref, a1_ref, f_ref, *, steps_per_seq):
    rows = h0_ref.shape[0]
    n_chunks = wdn_ref.shape[0] // FFN_COLS
    i = pl.program_id(0) % steps_per_seq

    def with_halo(prev_ref, main_ref, next_ref):
        return jnp.concatenate([prev_ref[...], main_ref[...], next_ref[...]], axis=0)

    y = (_dot(with_halo(ygp_ref, yg_ref, ygn_ref), wout_ref[:G_WIDTH])
         + _dot(with_halo(ydp_ref, yd_ref, ydn_ref), wout_ref[G_WIDTH:G_WIDTH + DA_WIDTH])
         + _dot(with_halo(ymp_ref, ym_ref, ymn_ref), wout_ref[G_WIDTH + DA_WIDTH:]))
    h1e = _layer_norm(ALPHA * with_halo(h0p_ref, h0_ref, h0n_ref) + y, g1_ref[...], b1_ref[...])
    h1_ref[...] = h1e[HALO:HALO + rows]
    prev = h1e[:HALO].astype(BF16)
    nxt = h1e[HALO + rows:].astype(BF16)
    hext_ref[:HALO] = jnp.where(i == 0, jnp.zeros_like(prev), prev)
    hext_ref[HALO:HALO + rows] = h1e[HALO:HALO + rows].astype(BF16)
    hext_ref[HALO + rows:] = jnp.where(i == steps_per_seq - 1, jnp.zeros_like(nxt), nxt)

    def columns(c):
        gate = pl.ds(pl.multiple_of(c * FFN_COLS, FFN_COLS), FFN_COLS)
        val = pl.ds(pl.multiple_of(D_FF + c * FFN_COLS, FFN_COLS), FFN_COLS)
        return gate, val

    def stage(up, down, first=False):
        ext_rows = rows + 2 * HALO
        up_rows = ext_rows // _FFN_SPLIT
        down_rows = rows // (2 * _FFN_SPLIT)
        if up is not None:
            up_cols = columns(up[0])

            def project(piece):
                half, part = divmod(piece, _FFN_SPLIT)
                r0 = part * up_rows
                up[1][half, r0:r0 + up_rows, :] = _dot(hext_ref[r0:r0 + up_rows, :],
                                                       wup_ref[:, up_cols[half]])

        if down is not None:
            c, a_ref = down
            down_cols = columns(c)
            w = wdn_ref[pl.ds(pl.multiple_of(c * FFN_COLS, FFN_COLS), FFN_COLS), :]

            def conv(half, r0):
                cols = down_cols[half]
                cw = cw_ref[:, cols]
                lo = HALO + r0
                return (cw[0:1] * a_ref[half, lo - 1:lo - 1 + down_rows, :]
                        + cw[1:2] * a_ref[half, lo:lo + down_rows, :]
                        + cw[2:3] * a_ref[half, lo + 1:lo + 1 + down_rows, :]
                        + cb_ref[:, cols])

            def contract(piece):
                r0 = piece * down_rows
                act = (_gelu(conv(0, r0)) * conv(1, r0)).astype(BF16)
                if first:
                    f_ref[r0:r0 + down_rows, :] = _dot(act, w)
                else:
                    f_ref[r0:r0 + down_rows, :] += _dot(act, w)

        for piece in range(2 * _FFN_SPLIT):
            if up is not None:
                project(piece)
            if down is not None:
                contract(piece)

    stage((0, a0_ref), None)
    stage((1, a1_ref), (0, a0_ref), first=True)

    def group(p, carry):
        c = _FFN_UNROLL * p + 1
        for u in range(0, _FFN_UNROLL, 2):
            stage((c + u + 1, a0_ref), (c + u, a1_ref))
            stage((c + u + 2, a1_ref), (c + u + 1, a0_ref))
        return carry

    lax.fori_loop(0, (n_chunks - 3) // _FFN_UNROLL, group, 0)
    stage((n_chunks - 1, a0_ref), (n_chunks - 2, a1_ref))
    stage(None, (n_chunks - 1, a0_ref))
    o_ref[...] = _layer_norm(ALPHA * h1_ref[...] + f_ref[...], g_ref[...], b_ref[...])


def _proj_ffn(h0, yg, yd, ym, w_out, g1, b1, w_up, conv_w, conv_b, w_down, g2, b2, seq_len):
    N, D = h0.shape
    R = FFN_ROWS
    halo_per_step = R // HALO
    n_halo = N // HALO
    n_chunks = D_FF // FFN_COLS
    assert n_chunks >= 3 and (n_chunks - 3) % _FFN_UNROLL == 0
    const = lambda shape: pl.BlockSpec(shape, lambda i: (0, 0))

    def with_halo(width):
        return [
            pl.BlockSpec((HALO, width), lambda i: (jnp.maximum(i * halo_per_step - 1, 0), 0)),
            pl.BlockSpec((R, width), lambda i: (i, 0)),
            pl.BlockSpec((HALO, width),
                         lambda i: (jnp.minimum((i + 1) * halo_per_step, n_halo - 1), 0)),
        ]

    return pl.pallas_call(
        functools.partial(_proj_ffn_kernel, steps_per_seq=seq_len // R),
        grid=(N // R,),
        in_specs=(with_halo(D) + with_halo(G_WIDTH) + with_halo(DA_WIDTH) + with_halo(M_WIDTH)
                  + [const((D, D)), const((1, D)), const((1, D)),
                     const((D, 2 * D_FF)), const((CONV_W, 2 * D_FF)), const((1, 2 * D_FF)),
                     const((D_FF, D)), const((1, D)), const((1, D))]),
        out_specs=pl.BlockSpec((R, D), lambda i: (i, 0)),
        out_shape=jax.ShapeDtypeStruct((N, D), F32),
        scratch_shapes=[
            pltpu.VMEM((R + 2 * HALO, D), BF16),
            pltpu.VMEM((R, D), F32),
            pltpu.VMEM((2, R + 2 * HALO, FFN_COLS), F32),
            pltpu.VMEM((2, R + 2 * HALO, FFN_COLS), F32),
            pltpu.VMEM((R, D), F32),
        ],
        compiler_params=_params(("arbitrary",)),
        name="proj_ffn",
    )(h0, h0, h0, yg, yg, yg, yd, yd, yd, ym, ym, ym, w_out, g1, b1,
      w_up, conv_w, conv_b, w_down, g2, b2)


def _alibi_slopes(n):
    return jnp.asarray([2.0 ** (-8.0 * (i + 1) / n) for i in range(n)], F32)


def kernel(x, mem, ln_emb_g, ln_emb_b, w_in, gmlp_ln_g, gmlp_ln_b, gmlp_ws, gmlp_bs, lambda_q1, lambda_k1, lambda_q2, lambda_k2, da_subln_g, mem_ln_g, mem_ln_b, w_mem_kv, w_out, ln1_g, ln1_b, w_up, conv_w, conv_b, w_down, ln2_g, ln2_b):
    B, T, D = x.shape
    assert DEPTH == 1 and w_in.shape[0] == 1
    assert T % PROJ_ROWS == 0 and T % ATT_Q == 0 and T % ATT_K == 0 and T % FFN_ROWS == 0
    assert D_FF % FFN_COLS == 0
    l = 0
    lambda_init = 0.8 - 0.6 * math.exp(-0.3 * l)
    row = lambda a: a.reshape(1, -1)

    w = w_in[l]
    u_end, v_end = G_WIDTH, 2 * G_WIDTH
    q_end = v_end + DA_QK
    k_end = q_end + DA_QK
    vd_end = k_end + DA_WIDTH
    w_a = jnp.concatenate([w[:, :v_end], w[:, q_end:k_end], w[:, vd_end:]], axis=1).astype(BF16)
    w_bT = jnp.concatenate([w[:, v_end:q_end], w[:, k_end:vd_end]], axis=1).T.astype(BF16)
    w_kmT = w_mem_kv[l][:, :M_WIDTH].T.astype(BF16)
    w_vm = w_mem_kv[l][:, M_WIDTH:].astype(BF16)
    ws = gmlp_ws[l].reshape(G_HEADS * CHUNK, CHUNK).astype(BF16)
    bs_tile = jnp.repeat(gmlp_bs[l].T, G_DIM, axis=1)

    h0, yg, ym, qT, k, vT = _in_proj(
        x, row(ln_emb_g), row(ln_emb_b), w_a, w_bT, row(gmlp_ln_g[l]), row(gmlp_ln_b[l]),
        ws, bs_tile, mem, row(mem_ln_g[l]), row(mem_ln_b[l]), w_kmT, w_vm)
    yd = _diff_attn(_alibi_slopes(DA_HEADS), qT, k, vT, row(lambda_q1[l]), row(lambda_k1[l]),
                    row(lambda_q2[l]), row(lambda_k2[l]), row(da_subln_g[l]), lambda_init)

    N = B * T
    out = _proj_ffn(h0.reshape(N, D), yg.reshape(N, G_WIDTH), yd.reshape(N, DA_WIDTH),
                    ym.reshape(N, M_WIDTH), w_out[l].astype(BF16), row(ln1_g[l]), row(ln1_b[l]),
                    w_up[l].astype(BF16), conv_w[l], row(conv_b[l]), w_down[l].astype(BF16),
                    row(ln2_g[l]), row(ln2_b[l]), T)
    return out.reshape(B, T, D)
```
